```python
import jax, jax.numpy as jnp
from jax import lax
import numpy as np

D_MODEL = 1024
BATCH = 8
SEQ = 4096
DEPTH = 2
DEC_BATCH = 128
DEC_SEQ = 1
PAST_LEN = 16384
PAGE_SIZE = 128

RET_HEADS = 4
RET_DK = 64
RET_DV = 64
RET_CHUNK = 128
MLA_HEADS = 8
MLA_NOPE = 64
MLA_ROPE = 32
MLA_V = 64
MLA_Q_RANK = 256
MLA_KV_RANK = 128
MLA_Q_BLOCK = 128
MLA_SCALE = (MLA_NOPE + MLA_ROPE) ** -0.5
RG_WIDTH = 256
RG_BLOCKS = 4
RG_CONV = 4
RG_C = 8.0
D_FF = 2816
FFN_CONV = 3
ROPE_BASE = 10000.0
EPS = 1e-6
NEG_INF = -1e30
MIX_WIDTH = RET_HEADS * RET_DV + MLA_HEADS * MLA_V + RG_WIDTH
IN_SIZES = (RET_HEADS * RET_DK, RET_HEADS * RET_DK, RET_HEADS * RET_DV, RET_HEADS * RET_DV,
            MLA_Q_RANK, MLA_KV_RANK, MLA_ROPE, RG_WIDTH, RG_WIDTH)
N_IN = sum(IN_SIZES)

kernel_name = 'hymba_retention_mla_rglru_convffn_step'


def rmsnorm(x, w):
    xf = x.astype(jnp.float32)
    y = xf * lax.rsqrt(jnp.mean(xf * xf, axis=-1, keepdims=True) + EPS)
    return (y * w.astype(jnp.float32)).astype(x.dtype)


def rope(x, pos):
    half = x.shape[-1] // 2
    freqs = ROPE_BASE ** (-jnp.arange(half, dtype=jnp.float32) / half)
    ang = pos[:, None] * freqs[None, :]
    cos = jnp.cos(ang)[:, None, :]
    sin = jnp.sin(ang)[:, None, :]
    xf = x.astype(jnp.float32)
    x1, x2 = xf[..., :half], xf[..., half:]
    return jnp.concatenate([x1 * cos - x2 * sin, x1 * sin + x2 * cos], axis=-1).astype(x.dtype)


def causal_dwconv(buf, x, w, b):
    width = w.shape[0]
    T = x.shape[1]
    xp = jnp.concatenate([buf.astype(x.dtype), x], axis=1)
    y = b + xp[:, 0:T] * w[0]
    for k in range(1, width):
        y = y + xp[:, k:k + T] * w[k]
    return y, xp[:, xp.shape[1] - (width - 1):]


def retention_log_gamma():
    return jnp.log(1.0 - 2.0 ** (-5.0 - jnp.arange(RET_HEADS, dtype=jnp.float32)))


def retention_chunk(q, k, v, S0, log_gamma):
    L = q.shape[1]
    idx = jnp.arange(L, dtype=jnp.float32)
    rel = idx[:, None] - idx[None, :]
    decay = jnp.where(rel >= 0, jnp.exp(log_gamma[:, None, None] * jnp.maximum(rel, 0.0)), 0.0)
    s = jnp.einsum('blhd,bmhd->bhlm', q, k) * decay
    inner = jnp.einsum('bhlm,bmhe->blhe', s, v)
    cross = jnp.einsum('blhd,bhde->blhe', q, S0) * jnp.exp(log_gamma[None, :] * (idx[:, None] + 1.0))[None, :, :, None]
    wk = jnp.exp(log_gamma[None, :] * (L - 1.0 - idx[:, None]))
    S = jnp.exp(log_gamma * L)[None, :, None, None] * S0 + jnp.einsum('blhd,blhe,lh->bhde', k, v, wk)
    return inner + cross, S


def retention_mixer(q, k, v, S0, pos):
    B, T = q.shape[0], q.shape[1]
    q = rope(q, pos).astype(jnp.float32)
    k = rope(k, pos).astype(jnp.float32) * (RET_DK ** -0.5)
    v = v.astype(jnp.float32)
    S0 = S0.astype(jnp.float32)
    lg = retention_log_gamma()
    if T > RET_CHUNK and T % RET_CHUNK == 0:
        n = T // RET_CHUNK

        def to_chunks(a):
            return a.reshape(B, n, RET_CHUNK, *a.shape[2:]).swapaxes(0, 1)

        def step(S, qkv):
            o, S_next = retention_chunk(qkv[0], qkv[1], qkv[2], S, lg)
            return S_next, o

        S_fin, o = lax.scan(step, S0, (to_chunks(q), to_chunks(k), to_chunks(v)))
        o = o.swapaxes(0, 1).reshape(B, T, RET_HEADS, RET_DV)
    else:
        o, S_fin = retention_chunk(q, k, v, S0, lg)
    return o, S_fin


def retention_out(o, g, gn_w):
    B, T, H, E = o.shape
    mu = jnp.mean(o, axis=-1, keepdims=True)
    var = jnp.mean(jnp.square(o - mu), axis=-1, keepdims=True)
    on = ((o - mu) * lax.rsqrt(var + EPS)).reshape(B, T, H * E) * gn_w.astype(jnp.float32)
    return on.astype(g.dtype) * jax.nn.silu(g)


def mla_project(cq, ckv, kr, pos, q_norm_w, w_uq, kv_norm_w, w_uk):
    cq = rmsnorm(cq, q_norm_w)
    q = jnp.einsum('btr,rhd->bthd', cq, w_uq)
    q_nope, q_rope = q[..., :MLA_NOPE], q[..., MLA_NOPE:]
    q_rope = rope(q_rope, pos) * MLA_SCALE
    ckv = rmsnorm(ckv, kv_norm_w)
    k_rope = rope(kr[:, :, None, :], pos)[:, :, 0]
    q_lat = jnp.einsum('bthd,rhd->bthr', q_nope, w_uk) * MLA_SCALE
    return q_lat, q_rope, ckv, k_rope


def mla_attend_causal(q_lat, q_rope, ckv, k_rope):
    B, S, H, R = q_lat.shape
    nb = S // MLA_Q_BLOCK
    kpos = jnp.arange(S)

    def blk(args):
        ql, qr, start = args
        s = (jnp.einsum('bqhr,bkr->bhqk', ql, ckv) + jnp.einsum('bqhd,bkd->bhqk', qr, k_rope)).astype(jnp.float32)
        qpos = start + jnp.arange(MLA_Q_BLOCK)
        s = jnp.where(kpos[None, :] <= qpos[:, None], s, NEG_INF)
        p = jax.nn.softmax(s, axis=-1).astype(ckv.dtype)
        return jnp.einsum('bhqk,bkr->bqhr', p, ckv)

    qlb = q_lat.reshape(B, nb, MLA_Q_BLOCK, H, R).swapaxes(0, 1)
    qrb = q_rope.reshape(B, nb, MLA_Q_BLOCK, H, MLA_ROPE).swapaxes(0, 1)
    out = lax.map(blk, (qlb, qrb, jnp.arange(nb) * MLA_Q_BLOCK))
    return out.swapaxes(0, 1).reshape(B, S, H, R)


def mla_attend_paged(q_lat, q_rope, ckv, k_rope, pool_lat, pool_rope, page_table):
    DB, T, H, R = q_lat.shape
    past_lat = pool_lat[page_table].reshape(DB, -1, R)
    past_rope = pool_rope[page_table].reshape(DB, -1, MLA_ROPE)
    P = past_lat.shape[1]
    s_past = jnp.einsum('bqhr,bkr->bhqk', q_lat, past_lat) + jnp.einsum('bqhd,bkd->bhqk', q_rope, past_rope)
    s_new = jnp.einsum('bqhr,bkr->bhqk', q_lat, ckv) + jnp.einsum('bqhd,bkd->bhqk', q_rope, k_rope)
    causal = jnp.arange(T)[None, :] <= jnp.arange(T)[:, None]
    s_new = jnp.where(causal, s_new.astype(jnp.float32), NEG_INF)
    s = jnp.concatenate([s_past.astype(jnp.float32), s_new], axis=-1)
    p = jax.nn.softmax(s, axis=-1)
    p_past = p[..., :P].astype(ckv.dtype)
    p_new = p[..., P:].astype(ckv.dtype)
    return jnp.einsum('bhqk,bkr->bqhr', p_past, past_lat) + jnp.einsum('bhqk,bkr->bqhr', p_new, ckv)


def rglru_mixer(xb, gb, buf0, h0, conv_w, conv_b, w_a, b_a, w_x, b_x, lam):
    xc, buf = causal_dwconv(buf0, xb, conv_w, conv_b)
    B, T, _ = xc.shape
    xr = xc.reshape(B, T, RG_BLOCKS, RG_WIDTH // RG_BLOCKS)
    r = jax.nn.sigmoid((jnp.einsum('btnc,ncd->btnd', xr, w_a).reshape(B, T, RG_WIDTH) + b_a).astype(jnp.float32))
    i = jax.nn.sigmoid((jnp.einsum('btnc,ncd->btnd', xr, w_x).reshape(B, T, RG_WIDTH) + b_x).astype(jnp.float32))
    log_a = -RG_C * r * jax.nn.softplus(-lam.astype(jnp.float32))
    a = jnp.exp(log_a)
    u = jnp.sqrt(-jnp.expm1(2.0 * log_a)) * i * xc.astype(jnp.float32)
    u = u.at[:, 0].add(a[:, 0] * h0.astype(jnp.float32))

    def comb(left, right):
        return left[0] * right[0], right[0] * left[1] + right[1]

    _, h = lax.associative_scan(comb, (a, u), axis=1)
    out = h.astype(xb.dtype) * jax.nn.gelu(gb)
    return out, buf, h[:, -1]


def conv_ffn(h, buf0, w_up, conv_w, conv_b, w_down):
    up = h @ w_up
    gate, val = up[..., :D_FF], up[..., D_FF:]
    gc, buf = causal_dwconv(buf0, gate, conv_w, conv_b)
    return (jax.nn.silu(gc) * val) @ w_down, buf


def layer(x, pos, l, P, ret_S0, rg_buf0, rg_h0, ffn_buf0, attend):
    B, T, _ = x.shape
    h = rmsnorm(x, P['norm1_w'][l])
    z = h @ P['w_in'][l]
    rq, rk, rv, rgate, cq, ckv, kr, gx, gg = jnp.split(z, [int(c) for c in np.cumsum(IN_SIZES)[:-1]], axis=-1)
    o_ret, S_ret = retention_mixer(rq.reshape(B, T, RET_HEADS, RET_DK), rk.reshape(B, T, RET_HEADS, RET_DK),
                                   rv.reshape(B, T, RET_HEADS, RET_DV), ret_S0, pos)
    y_ret = retention_out(o_ret, rgate, P['ret_gn_w'][l])
    q_lat, q_rope, ckv, k_rope = mla_project(cq, ckv, kr, pos, P['mla_q_norm_w'][l], P['mla_w_uq'][l],
                                             P['mla_kv_norm_w'][l], P['mla_w_uk'][l])
    o_lat = attend(l, q_lat, q_rope, ckv, k_rope)
    y_mla = jnp.einsum('bthr,rhd->bthd', o_lat, P['mla_w_uv'][l]).reshape(B, T, MLA_HEADS * MLA_V)
    y_rg, rg_buf, rg_h = rglru_mixer(gx, gg, rg_buf0, rg_h0, P['rg_conv_w'][l], P['rg_conv_b'][l],
                                     P['rg_w_a'][l], P['rg_b_a'][l], P['rg_w_x'][l], P['rg_b_x'][l],
                                     P['rg_lambda'][l])
    x = x + jnp.concatenate([y_ret, y_mla, y_rg], axis=-1) @ P['w_out'][l]
    f, ffn_buf = conv_ffn(rmsnorm(x, P['norm2_w'][l]), ffn_buf0, P['ffn_w_up'][l], P['ffn_conv_w'][l],
                          P['ffn_conv_b'][l], P['ffn_w_down'][l])
    x = x + f
    return x, (ckv, k_rope, S_ret, rg_buf, rg_h, ffn_buf)


def run_trunk(x, pos, P, ret_S, rg_buf, rg_h, ffn_buf, attend):
    news = []
    for l in range(DEPTH):
        x, st = layer(x, pos, l, P, ret_S[l], rg_buf[l], rg_h[l], ffn_buf[l], attend)
        news.append(st)
    y = rmsnorm(x, P['final_norm_w'])
    stacked = [jnp.stack([s[i] for s in news]) for i in range(6)]
    return y, stacked


def setup_inputs(seed: int = 0) -> dict:
    key = jax.random.key(seed)
    ks = iter(jax.random.split(key, 40))
    n_pages = PAST_LEN // PAGE_SIZE
    n_phys = (DEC_BATCH * n_pages * 5) // 4

    def nrm(shape, scale=1.0):
        return jax.random.normal(next(ks), shape, jnp.float32) * scale

    x_prompt = nrm((BATCH, SEQ, D_MODEL))
    x_sample = nrm((DEC_BATCH, DEC_SEQ, D_MODEL))
    cache_mla_latent = nrm((DEPTH, n_phys, PAGE_SIZE, MLA_KV_RANK))
    cache_mla_krope = nrm((DEPTH, n_phys, PAGE_SIZE, MLA_ROPE))
    page_table = jax.random.permutation(next(ks), n_phys)[:DEC_BATCH * n_pages].reshape(DEC_BATCH, n_pages).astype(jnp.int32)
    state_ret = nrm((DEPTH, DEC_BATCH, RET_HEADS, RET_DK, RET_DV))
    state_rg_conv = nrm((DEPTH, DEC_BATCH, RG_CONV - 1, RG_WIDTH))
    state_rglru = nrm((DEPTH, DEC_BATCH, RG_WIDTH), 0.5)
    state_ffn_conv = nrm((DEPTH, DEC_BATCH, FFN_CONV - 1, D_FF))
    bw = RG_WIDTH // RG_BLOCKS
    u = jax.random.uniform(next(ks), (DEPTH, RG_WIDTH), jnp.float32, 0.9, 0.999)
    a_base = u ** (1.0 / RG_C)
    rg_lambda = jnp.log(a_base) - jnp.log1p(-a_base)
    return {
        'x_prompt': x_prompt,
        'x_sample': x_sample,
        'cache_mla_latent': cache_mla_latent,
        'cache_mla_krope': cache_mla_krope,
        'page_table': page_table,
        'state_ret': state_ret,
        'state_rg_conv': state_rg_conv,
        'state_rglru': state_rglru,
        'state_ffn_conv': state_ffn_conv,
        'norm1_w': 1.0 + nrm((DEPTH, D_MODEL), 0.01),
        'w_in': nrm((DEPTH, D_MODEL, N_IN), D_MODEL ** -0.5),
        'ret_gn_w': 1.0 + nrm((DEPTH, RET_HEADS * RET_DV), 0.01),
        'mla_q_norm_w': 1.0 + nrm((DEPTH, MLA_Q_RANK), 0.01),
        'mla_w_uq': nrm((DEPTH, MLA_Q_RANK, MLA_HEADS, MLA_NOPE + MLA_ROPE), MLA_Q_RANK ** -0.5),
        'mla_kv_norm_w': 1.0 + nrm((DEPTH, MLA_KV_RANK), 0.01),
        'mla_w_uk': nrm((DEPTH, MLA_KV_RANK, MLA_HEADS, MLA_NOPE), MLA_KV_RANK ** -0.5),
        'mla_w_uv': nrm((DEPTH, MLA_KV_RANK, MLA_HEADS, MLA_V), MLA_KV_RANK ** -0.5),
        'rg_conv_w': nrm((DEPTH, RG_CONV, RG_WIDTH), RG_CONV ** -0.5),
        'rg_conv_b': nrm((DEPTH, RG_WIDTH), 0.01),
        'rg_w_a': nrm((DEPTH, RG_BLOCKS, bw, bw), bw ** -0.5),
        'rg_b_a': nrm((DEPTH, RG_WIDTH), 0.01),
        'rg_w_x': nrm((DEPTH, RG_BLOCKS, bw, bw), bw ** -0.5),
        'rg_b_x': nrm((DEPTH, RG_WIDTH), 0.01),
        'rg_lambda': rg_lambda,
        'w_out': nrm((DEPTH, MIX_WIDTH, D_MODEL), MIX_WIDTH ** -0.5),
        'norm2_w': 1.0 + nrm((DEPTH, D_MODEL), 0.01),
        'ffn_w_up': nrm((DEPTH, D_MODEL, 2 * D_FF), D_MODEL ** -0.5),
        'ffn_conv_w': nrm((DEPTH, FFN_CONV, D_FF), FFN_CONV ** -0.5),
        'ffn_conv_b': nrm((DEPTH, D_FF), 0.01),
        'ffn_w_down': nrm((DEPTH, D_FF, D_MODEL), D_FF ** -0.5),
        'final_norm_w': 1.0 + nrm((D_MODEL,), 0.01),
    }


def reference(x_prompt, x_sample, cache_mla_latent, cache_mla_krope, page_table, state_ret, state_rg_conv,
              state_rglru, state_ffn_conv, norm1_w, w_in, ret_gn_w, mla_q_norm_w, mla_w_uq, mla_kv_norm_w,
              mla_w_uk, mla_w_uv, rg_conv_w, rg_conv_b, rg_w_a, rg_b_a, rg_w_x, rg_b_x, rg_lambda, w_out,
              norm2_w, ffn_w_up, ffn_conv_w, ffn_conv_b, ffn_w_down, final_norm_w):
    P = dict(norm1_w=norm1_w, w_in=w_in, ret_gn_w=ret_gn_w, mla_q_norm_w=mla_q_norm_w, mla_w_uq=mla_w_uq,
             mla_kv_norm_w=mla_kv_norm_w, mla_w_uk=mla_w_uk, mla_w_uv=mla_w_uv, rg_conv_w=rg_conv_w,
             rg_conv_b=rg_conv_b, rg_w_a=rg_w_a, rg_b_a=rg_b_a, rg_w_x=rg_w_x, rg_b_x=rg_b_x,
             rg_lambda=rg_lambda, w_out=w_out, norm2_w=norm2_w, ffn_w_up=ffn_w_up, ffn_conv_w=ffn_conv_w,
             ffn_conv_b=ffn_conv_b, ffn_w_down=ffn_w_down, final_norm_w=final_norm_w)

    B, S, _ = x_prompt.shape
    pos_p = jnp.arange(S, dtype=jnp.float32)
    zeros_ret = jnp.zeros((DEPTH, B, RET_HEADS, RET_DK, RET_DV), jnp.float32)
    zeros_rgc = jnp.zeros((DEPTH, B, RG_CONV - 1, RG_WIDTH), x_prompt.dtype)
    zeros_rgh = jnp.zeros((DEPTH, B, RG_WIDTH), jnp.float32)
    zeros_ffc = jnp.zeros((DEPTH, B, FFN_CONV - 1, D_FF), x_prompt.dtype)

    def attend_prompt(l, ql, qr, c, k):
        return mla_attend_causal(ql, qr, c, k)

    y_prompt, (p_lat, p_rope, p_ret, p_rgc, p_rgh, p_ffc) = run_trunk(
        x_prompt, pos_p, P, zeros_ret, zeros_rgc, zeros_rgh, zeros_ffc, attend_prompt)

    T = x_sample.shape[1]
    pos_s = PAST_LEN + jnp.arange(T, dtype=jnp.float32)

    def attend_sample(l, ql, qr, c, k):
        return mla_attend_paged(ql, qr, c, k, cache_mla_latent[l], cache_mla_krope[l], page_table)

    y_sample, (s_lat, s_rope, s_ret, s_rgc, s_rgh, s_ffc) = run_trunk(
        x_sample, pos_s, P, state_ret, state_rg_conv, state_rglru, state_ffn_conv, attend_sample)

    return (y_prompt, y_sample, p_lat, p_rope, p_ret, p_rgc, p_rgh, p_ffc,
            s_lat, s_rope, s_ret, s_rgc, s_rgh, s_ffc)
```

```python
import functools

import jax
import jax.numpy as jnp
from jax import lax
from jax.experimental import pallas as pl
from jax.experimental.pallas import tpu as pltpu

RET_HEADS = 4
RET_DK = 64
RET_DV = 64
RET_CHUNK = 128
MLA_HEADS = 8
MLA_NOPE = 64
MLA_ROPE = 32
MLA_V = 64
MLA_Q_RANK = 256
MLA_KV_RANK = 128
MLA_SCALE = (MLA_NOPE + MLA_ROPE) ** -0.5
RG_WIDTH = 256
RG_BLOCKS = 4
RG_CONV = 4
RG_C = 8.0
FFN_CONV = 3
ROPE_BASE = 10000.0
EPS = 1e-6
NEG_INF = -1e30

RET_W = RET_HEADS * RET_DK
QR_W = MLA_HEADS * MLA_ROPE
QN_W = MLA_HEADS * MLA_NOPE
KCAT_W = 256
LANE = 128
VMEM_LIMIT = 56 * 1024 * 1024

BF16 = jnp.bfloat16
F32 = jnp.float32

_OFF_RQ, _OFF_RK, _OFF_RV, _OFF_RG = 0, 256, 512, 768
_OFF_CQ, _OFF_CKV, _OFF_GX, _OFF_GG, _OFF_KR = 1024, 1280, 1408, 1664, 1920
_N_IN_PAD = 2048


def _params(n_axes):
    return pltpu.CompilerParams(dimension_semantics=("arbitrary",) * n_axes,
                                vmem_limit_bytes=VMEM_LIMIT)


def _const_spec(shape):
    n = len(shape)
    return pl.BlockSpec(shape, lambda *_: (0,) * n, pipeline_mode=pl.Buffered(1))


def _dot(a, b):
    return jnp.dot(a, b, preferred_element_type=F32)


def _dot_nt(a, b):
    return lax.dot_general(a, b, (((1,), (1,)), ((), ())), preferred_element_type=F32)


def _dot_tn(a, b):
    return lax.dot_general(a, b, (((0,), (0,)), ((), ())), preferred_element_type=F32)


def _rms(x, w):
    return x * lax.rsqrt(jnp.mean(x * x, axis=-1, keepdims=True) + EPS) * w


def _rope3(x, tab_ref, half):
    w = x.shape[-1]
    return (x * tab_ref[0] + pltpu.roll(x, w - half, 1) * tab_ref[1]
            + pltpu.roll(x, half, 1) * tab_ref[2])


def _inproj_kernel(x_ref, n1_ref, win_ref, tabr_ref, tabq_ref, tabk_ref, qnw_ref, wuqn_ref,
                   wuqr_ref, wukp_ref, kvnw_ref,
                   ret_ref, q_ref, kcat_ref, plat_ref, prope_ref, rg_ref):
    h = _rms(x_ref[...], n1_ref[...]).astype(BF16)
    z = _dot(h, win_ref[...])

    ret_ref[:, _OFF_RQ:_OFF_RQ + RET_W] = _rope3(z[:, _OFF_RQ:_OFF_RQ + RET_W], tabr_ref, RET_DK // 2)
    ret_ref[:, _OFF_RK:_OFF_RK + RET_W] = (
        _rope3(z[:, _OFF_RK:_OFF_RK + RET_W], tabr_ref, RET_DK // 2) * (RET_DK ** -0.5))
    ret_ref[:, _OFF_RV:_OFF_CQ] = z[:, _OFF_RV:_OFF_CQ]
    rg_ref[...] = z[:, _OFF_GX:_OFF_KR]

    ckvn = _rms(z[:, _OFF_CKV:_OFF_CKV + MLA_KV_RANK], kvnw_ref[...])
    plat_ref[...] = ckvn
    krr = _rope3(z[:, _OFF_KR:_OFF_KR + LANE], tabk_ref, MLA_ROPE // 2)
    prope_ref[...] = krr[:, :MLA_ROPE]
    kcat_ref[:, 0:MLA_KV_RANK] = ckvn.astype(BF16)
    kcat_ref[:, MLA_KV_RANK:KCAT_W] = krr.astype(BF16)

    cqn = _rms(z[:, _OFF_CQ:_OFF_CQ + MLA_Q_RANK], qnw_ref[...]).astype(BF16)
    qn = _dot(cqn, wuqn_ref[...]).astype(BF16)
    qr = _rope3(_dot(cqn, wuqr_ref[...]), tabq_ref, MLA_ROPE // 2) * MLA_SCALE
    lane = lax.broadcasted_iota(jnp.int32, (1, LANE), 1)
    heads_per_tile = LANE // MLA_ROPE
    for p in range(MLA_HEADS // 2):
        ql = _dot(qn[:, p * LANE:(p + 1) * LANE], wukp_ref[p]) * MLA_SCALE
        for e in range(2):
            hd = 2 * p + e
            q_ref[hd, :, 0:MLA_KV_RANK] = ql[:, e * LANE:(e + 1) * LANE].astype(BF16)
            tile = qr[:, (hd // heads_per_tile) * LANE:(hd // heads_per_tile + 1) * LANE]
            sh = MLA_ROPE * (hd % heads_per_tile)
            if sh:
                tile = pltpu.roll(tile, LANE - sh, 1)
            q_ref[hd, :, MLA_KV_RANK:KCAT_W] = jnp.where(lane < MLA_ROPE, tile, 0.0).astype(BF16)


def _inproj(x2d, n_batch, tm, lw, tabs):
    rows, d = x2d.shape
    ns = rows // (n_batch * tm)
    tabr, tabq, tabk = tabs
    row_map = lambda s, b: (b * ns + s, 0)
    tab_map = lambda s, b: (0, s, 0)
    out_shape = (
        jax.ShapeDtypeStruct((rows, 4 * RET_W), F32),
        jax.ShapeDtypeStruct((MLA_HEADS, rows, KCAT_W), BF16),
        jax.ShapeDtypeStruct((rows, KCAT_W), BF16),
        jax.ShapeDtypeStruct((rows, MLA_KV_RANK), F32),
        jax.ShapeDtypeStruct((rows, MLA_ROPE), F32),
        jax.ShapeDtypeStruct((rows, 2 * RG_WIDTH), F32),
    )
    return pl.pallas_call(
        _inproj_kernel,
        grid=(ns, n_batch),
        in_specs=[
            pl.BlockSpec((tm, d), row_map),
            _const_spec((1, d)),
            _const_spec((d, _N_IN_PAD)),
            pl.BlockSpec((3, tm, RET_W), tab_map),
            pl.BlockSpec((3, tm, QR_W), tab_map),
            pl.BlockSpec((3, tm, LANE), tab_map),
            _const_spec((1, MLA_Q_RANK)),
            _const_spec((MLA_Q_RANK, QN_W)),
            _const_spec((MLA_Q_RANK, QR_W)),
            _const_spec((MLA_HEADS // 2, LANE, 2 * MLA_KV_RANK)),
            _const_spec((1, MLA_KV_RANK)),
        ],
        out_specs=[
            pl.BlockSpec((tm, 4 * RET_W), row_map),
            pl.BlockSpec((MLA_HEADS, tm, KCAT_W), lambda s, b: (0, b * ns + s, 0)),
            pl.BlockSpec((tm, KCAT_W), row_map),
            pl.BlockSpec((tm, MLA_KV_RANK), row_map),
            pl.BlockSpec((tm, MLA_ROPE), row_map),
            pl.BlockSpec((tm, 2 * RG_WIDTH), row_map),
        ],
        out_shape=out_shape,
        compiler_params=_params(2),
        name="inproj",
    )(x2d, lw["norm1_w"], lw["w_in"], tabr, tabq, tabk, lw["q_norm_w"], lw["w_uq_nope"],
      lw["w_uq_rope"], lw["w_uk_pair"], lw["kv_norm_w"])


def _group_norm_gate(o, g, gnw, avg):
    mu = _dot(o, avg)
    d = o - mu
    var = _dot(d * d, avg)
    return d * lax.rsqrt(var + EPS) * gnw * (g * jax.nn.sigmoid(g))


def _ret_kernel(ret_ref, dec_ref, gq_ref, wk_ref, gl_ref, bd_ref, gnw_ref, avg_ref,
                y_ref, sbd_ref, s_scr):
    @pl.when(pl.program_id(1) == 0)
    def _():
        s_scr[...] = jnp.zeros_like(s_scr)

    q = ret_ref[:, _OFF_RQ:_OFF_RQ + RET_W]
    k = ret_ref[:, _OFF_RK:_OFF_RK + RET_W]
    vb = ret_ref[:, _OFF_RV:_OFF_RV + RET_W].astype(BF16)
    g = ret_ref[:, _OFF_RG:_OFF_RG + RET_W]
    kb = k.astype(BF16)
    s_old = s_scr[...]
    head_of_lane = lax.broadcasted_iota(jnp.int32, (1, RET_W), 1) // RET_DK
    o = _dot(q.astype(BF16), s_old.astype(BF16)) * gq_ref[...]
    for h in range(RET_HEADS):
        mh = head_of_lane == h
        s = _dot_nt(jnp.where(mh, q, 0.0).astype(BF16), kb) * dec_ref[h]
        o = o + jnp.where(mh, _dot(s.astype(BF16), vb), 0.0)
    upd = _dot_tn((k * wk_ref[...]).astype(BF16), vb)
    s_new = gl_ref[...] * s_old + bd_ref[...] * upd
    s_scr[...] = s_new
    sbd_ref[...] = s_new
    y_ref[...] = _group_norm_gate(o, g, gnw_ref[...], avg_ref[...])


def _retention_prompt(ret_in, n_batch, lw, rt):
    rows = ret_in.shape[0]
    L = rt["chunk"]
    nc = rows // (n_batch * L)
    return pl.pallas_call(
        _ret_kernel,
        grid=(n_batch, nc),
        in_specs=[
            pl.BlockSpec((L, 4 * RET_W), lambda b, c: (b * nc + c, 0)),
            _const_spec((RET_HEADS, L, L)),
            _const_spec((L, RET_W)),
            _const_spec((L, RET_W)),
            _const_spec((RET_W, RET_W)),
            _const_spec((RET_W, RET_W)),
            _const_spec((1, RET_W)),
            _const_spec((RET_W, RET_W)),
        ],
        out_specs=[
            pl.BlockSpec((L, RET_W), lambda b, c: (b * nc + c, 0)),
            pl.BlockSpec((None, RET_W, RET_W), lambda b, c: (b, 0, 0)),
        ],
        out_shape=(jax.ShapeDtypeStruct((rows, RET_W), F32),
                   jax.ShapeDtypeStruct((n_batch, RET_W, RET_W), F32)),
        scratch_shapes=[pltpu.VMEM((RET_W, RET_W), F32)],
        compiler_params=_params(2),
        name="retention",
    )(ret_in, rt["decay"], rt["gq"], rt["wk"], rt["gl"], rt["bd"], lw["ret_gn_w"], rt["avg"])


def _ret_step_kernel(q_ref, k_ref, v_ref, g_ref, s0_ref, gam_ref, gnw_ref, y_ref, s_ref):
    s_new = gam_ref[...] * s0_ref[...] + k_ref[...] * v_ref[...]
    s_ref[...] = s_new
    o = jnp.sum(q_ref[...] * s_new, axis=2, keepdims=True)
    mu = jnp.mean(o, axis=3, keepdims=True)
    d = o - mu
    var = jnp.mean(d * d, axis=3, keepdims=True)
    g = g_ref[...]
    y_ref[...] = d * lax.rsqrt(var + EPS) * gnw_ref[...] * (g * jax.nn.sigmoid(g))


def _retention_step(ret_in, s0, lw, rt):
    n = ret_in.shape[0]
    bs = 8
    q4 = ret_in[:, _OFF_RQ:_OFF_RQ + RET_W].reshape(n, RET_HEADS, RET_DK, 1)
    k4 = ret_in[:, _OFF_RK:_OFF_RK + RET_W].reshape(n, RET_HEADS, RET_DK, 1)
    v4 = ret_in[:, _OFF_RV:_OFF_RV + RET_W].reshape(n, RET_HEADS, 1, RET_DV)
    g4 = ret_in[:, _OFF_RG:_OFF_RG + RET_W].reshape(n, RET_HEADS, 1, RET_DV)
    col = pl.BlockSpec((bs, RET_HEADS, RET_DK, 1), lambda i: (i, 0, 0, 0))
    row = pl.BlockSpec((bs, RET_HEADS, 1, RET_DV), lambda i: (i, 0, 0, 0))
    mat = pl.BlockSpec((bs, RET_HEADS, RET_DK, RET_DV), lambda i: (i, 0, 0, 0))
    y4, s_new = pl.pallas_call(
        _ret_step_kernel,
        grid=(n // bs,),
        in_specs=[col, col, row, row, mat,
                  _const_spec((1, RET_HEADS, RET_DK, RET_DV)),
                  _const_spec((1, RET_HEADS, 1, RET_DV))],
        out_specs=[row, mat],
        out_shape=(jax.ShapeDtypeStruct((n, RET_HEADS, 1, RET_DV), F32),
                   jax.ShapeDtypeStruct((n, RET_HEADS, RET_DK, RET_DV), F32)),
        compiler_params=_params(1),
        name="retention_step",
    )(q4, k4, v4, g4, s0, rt["gam4"], lw["ret_gn_w"].reshape(1, RET_HEADS, 1, RET_DV))
    return y4.reshape(n, RET_W), s_new


def _rg_gates(xc, wa_ref, ba_ref, wx_ref, bx_ref, lam_ref):
    xb = xc.astype(BF16)
    r = jax.nn.sigmoid(_dot(xb, wa_ref[...]) + ba_ref[...])
    i = jax.nn.sigmoid(_dot(xb, wx_ref[...]) + bx_ref[...])
    y = -lam_ref[...]
    softplus = jnp.maximum(y, 0.0) + jnp.log1p(jnp.exp(-jnp.abs(y)))
    log_a = -RG_C * r * softplus
    a = jnp.exp(log_a)
    u = jnp.sqrt(1.0 - jnp.exp(2.0 * log_a)) * i * xc
    return a, u


def _rg_kernel(rg_ref, cw_ref, cb_ref, wa_ref, ba_ref, wx_ref, bx_ref, lam_ref,
               y_ref, buf_ref, hlast_ref, xs_scr, h_scr):
    tt = rg_ref.shape[0]
    halo = 8

    @pl.when(pl.program_id(1) == 0)
    def _():
        xs_scr[0:halo, :] = jnp.zeros((halo, RG_WIDTH), F32)
        h_scr[...] = jnp.zeros_like(h_scr)

    gx = rg_ref[:, 0:RG_WIDTH]
    gg = rg_ref[:, RG_WIDTH:2 * RG_WIDTH]
    xs_scr[halo:halo + tt, :] = gx
    xc = cb_ref[...] + gx * cw_ref[RG_CONV - 1:RG_CONV, :]
    for j in range(1, RG_CONV):
        xc = xc + xs_scr[halo - j:halo - j + tt, :] * cw_ref[RG_CONV - 1 - j:RG_CONV - j, :]
    xs_scr[0:halo, :] = xs_scr[tt:tt + halo, :]
    buf_ref[...] = gx[tt - (RG_CONV - 1):tt, :]

    a, u = _rg_gates(xc, wa_ref, ba_ref, wx_ref, bx_ref, lam_ref)
    row = lax.broadcasted_iota(jnp.int32, (tt, 1), 0)
    step = 1
    while step < tt:
        keep = row >= step
        a_sh = jnp.where(keep, pltpu.roll(a, step, 0), 1.0)
        u_sh = jnp.where(keep, pltpu.roll(u, step, 0), 0.0)
        u = a * u_sh + u
        a = a * a_sh
        step *= 2
    h = u + a * h_scr[...]
    h_scr[...] = h[tt - 1:tt, :]
    hlast_ref[...] = h[tt - 1:tt, :]
    y_ref[...] = h * jax.nn.gelu(gg, approximate=True)


def _rglru_prompt(rg_in, n_batch, tt, lw):
    rows = rg_in.shape[0]
    nt = rows // (n_batch * tt)
    w = RG_WIDTH
    return pl.pallas_call(
        _rg_kernel,
        grid=(n_batch, nt),
        in_specs=[
            pl.BlockSpec((tt, 2 * w), lambda b, t: (b * nt + t, 0)),
            _const_spec((RG_CONV, w)), _const_spec((1, w)),
            _const_spec((w, w)), _const_spec((1, w)),
            _const_spec((w, w)), _const_spec((1, w)), _const_spec((1, w)),
        ],
        out_specs=[
            pl.BlockSpec((tt, w), lambda b, t: (b * nt + t, 0)),
            pl.BlockSpec((None, RG_CONV - 1, w), lambda b, t: (b, 0, 0)),
            pl.BlockSpec((None, 1, w), lambda b, t: (b, 0, 0)),
        ],
        out_shape=(jax.ShapeDtypeStruct((rows, w), F32),
                   jax.ShapeDtypeStruct((n_batch, RG_CONV - 1, w), F32),
                   jax.ShapeDtypeStruct((n_batch, 1, w), F32)),
        scratch_shapes=[pltpu.VMEM((tt + 8, w), F32), pltpu.VMEM((1, w), F32)],
        compiler_params=_params(2),
        name="rglru",
    )(rg_in, lw["rg_conv_w"], lw["rg_conv_b"], lw["rg_wa_bd"], lw["rg_b_a"], lw["rg_wx_bd"],
      lw["rg_b_x"], lw["rg_lambda"])


def _rg_step_kernel(rg_ref, b0_ref, b1_ref, b2_ref, h0_ref, cw_ref, cb_ref, wa_ref, ba_ref,
                    wx_ref, bx_ref, lam_ref, y_ref, h_ref):
    gx = rg_ref[:, 0:RG_WIDTH]
    gg = rg_ref[:, RG_WIDTH:2 * RG_WIDTH]
    xc = (cb_ref[...] + b0_ref[...] * cw_ref[0:1, :] + b1_ref[...] * cw_ref[1:2, :]
          + b2_ref[...] * cw_ref[2:3, :] + gx * cw_ref[3:4, :])
    a, u = _rg_gates(xc, wa_ref, ba_ref, wx_ref, bx_ref, lam_ref)
    h = u + a * h0_ref[...]
    h_ref[...] = h
    y_ref[...] = h * jax.nn.gelu(gg, approximate=True)


def _rglru_step(rg_in, buf0, h0, lw):
    n = rg_in.shape[0]
    w = RG_WIDTH
    full = lambda shape: pl.BlockSpec(shape, lambda i: (0,) * len(shape))
    return pl.pallas_call(
        _rg_step_kernel,
        grid=(1,),
        in_specs=[full((n, 2 * w)), full((n, w)), full((n, w)), full((n, w)), full((n, w)),
                  full((RG_CONV, w)), full((1, w)), full((w, w)), full((1, w)), full((w, w)),
                  full((1, w)), full((1, w))],
        out_specs=[full((n, w)), full((n, w))],
        out_shape=(jax.ShapeDtypeStruct((n, w), F32), jax.ShapeDtypeStruct((n, w), F32)),
        compiler_params=_params(1),
        name="rglru_step",
    )(rg_in, buf0[:, 0], buf0[:, 1], buf0[:, 2], h0, lw["rg_conv_w"], lw["rg_conv_b"],
      lw["rg_wa_bd"], lw["rg_b_a"], lw["rg_wx_bd"], lw["rg_b_x"], lw["rg_lambda"])


def _uv_project(o, tq, wuv_ref, out_ref):
    for p in range(MLA_HEADS // 2):
        pair = jnp.concatenate([o[(2 * p) * tq:(2 * p + 1) * tq], o[(2 * p + 1) * tq:(2 * p + 2) * tq]],
                               axis=1).astype(BF16)
        out_ref[:, p * LANE:(p + 1) * LANE] = _dot(pair, wuv_ref[p])


def _attn_kernel(q_ref, k_ref, wuv_ref, o_ref, m_scr, l_scr, acc_scr):
    tq = q_ref.shape[1]
    kb = tq
    m_rows = MLA_HEADS * tq
    i = pl.program_id(1)
    q = q_ref[...].reshape(m_rows, KCAT_W)
    m_scr[...] = jnp.full_like(m_scr, NEG_INF)
    l_scr[...] = jnp.zeros_like(l_scr)
    acc_scr[...] = jnp.zeros_like(acc_scr)

    def block(j, masked):
        kblk = k_ref[pl.ds(pl.multiple_of(j * kb, kb), kb), :]
        s = _dot_nt(q, kblk)
        if masked:
            qpos = lax.broadcasted_iota(jnp.int32, (MLA_HEADS, tq, kb), 1).reshape(m_rows, kb)
            kpos = lax.broadcasted_iota(jnp.int32, (m_rows, kb), 1)
            s = jnp.where(kpos <= qpos, s, NEG_INF)
        m_prev = m_scr[...]
        m_next = jnp.maximum(m_prev, jnp.max(s, axis=1, keepdims=True))
        p = jnp.exp(s - jnp.concatenate([m_next] * (kb // LANE), axis=1))
        alpha = jnp.exp(m_prev - m_next)
        l_scr[...] = alpha * l_scr[...] + jnp.sum(p, axis=1, keepdims=True)
        acc_scr[...] = alpha * acc_scr[...] + _dot(p.astype(BF16), kblk[:, 0:MLA_KV_RANK])
        m_scr[...] = m_next

    def body(j, carry):
        block(j, False)
        return carry

    lax.fori_loop(0, i, body, 0)
    block(i, True)
    o = acc_scr[...] / l_scr[...]
    _uv_project(o, tq, wuv_ref, o_ref)


def _attention_prompt(qh, kcat, n_batch, tq, lw):
    rows = kcat.shape[0]
    s_len = rows // n_batch
    nq = s_len // tq
    m_rows = MLA_HEADS * tq
    return pl.pallas_call(
        _attn_kernel,
        grid=(n_batch, nq),
        in_specs=[
            pl.BlockSpec((MLA_HEADS, tq, KCAT_W), lambda b, i: (0, b * nq + i, 0)),
            pl.BlockSpec((s_len, KCAT_W), lambda b, i: (b, 0)),
            _const_spec((MLA_HEADS // 2, 2 * MLA_KV_RANK, LANE)),
        ],
        out_specs=pl.BlockSpec((tq, MLA_HEADS * MLA_V), lambda b, i: (b * nq + i, 0)),
        out_shape=jax.ShapeDtypeStruct((rows, MLA_HEADS * MLA_V), F32),
        scratch_shapes=[pltpu.VMEM((m_rows, LANE), F32), pltpu.VMEM((m_rows, LANE), F32),
                        pltpu.VMEM((m_rows, MLA_KV_RANK), F32)],
        compiler_params=_params(2),
        name="attention",
    )(qh, kcat, lw["w_uv_pair"])


def _paged_kernel(pt_ref, q_ref, kself_ref, *refs, pages_per_step):
    del pt_ref
    lat_refs = refs[:pages_per_step]
    kr_refs = refs[pages_per_step:2 * pages_per_step]
    o_ref, m_scr, l_scr, acc_scr = refs[2 * pages_per_step:]
    s_idx = pl.program_id(1)

    @pl.when(s_idx == 0)
    def _():
        m_scr[...] = jnp.full_like(m_scr, NEG_INF)
        l_scr[...] = jnp.zeros_like(l_scr)
        acc_scr[...] = jnp.zeros_like(acc_scr)

    q = q_ref[...]
    q_lat = q[:, 0:MLA_KV_RANK]
    q_rope = q[:, MLA_KV_RANK:MLA_KV_RANK + MLA_ROPE]
    lats = [r[...].astype(BF16) for r in lat_refs]
    scores = [_dot_nt(q_lat, lat) + _dot_nt(q_rope, kr[...].astype(BF16))
              for lat, kr in zip(lats, kr_refs)]
    m_prev = m_scr[...]
    m_cur = functools.reduce(jnp.maximum, [jnp.max(s, axis=1, keepdims=True) for s in scores])
    m_next = jnp.maximum(m_prev, m_cur)
    alpha = jnp.exp(m_prev - m_next)
    l_new = alpha * l_scr[...]
    acc = alpha * acc_scr[...]
    for s, lat in zip(scores, lats):
        p = jnp.exp(s - m_next)
        l_new = l_new + jnp.sum(p, axis=1, keepdims=True)
        acc = acc + _dot(p.astype(BF16), lat)
    m_scr[...] = m_next
    l_scr[...] = l_new
    acc_scr[...] = acc

    @pl.when(s_idx == pl.num_programs(1) - 1)
    def _():
        kself = kself_ref[...].astype(F32)
        s_self = jnp.sum(q.astype(F32) * kself, axis=1, keepdims=True)
        m_prev = m_scr[...]
        m_fin = jnp.maximum(m_prev, s_self)
        alpha = jnp.exp(m_prev - m_fin)
        p_self = jnp.exp(s_self - m_fin)
        l_fin = alpha * l_scr[...] + p_self
        acc_fin = alpha * acc_scr[...] + p_self * kself[:, 0:MLA_KV_RANK]
        o_ref[...] = acc_fin / l_fin


def _attention_paged(qh, kcat, pool_lat, pool_rope, page_table, layer, pages_per_step):
    n = kcat.shape[0]
    n_pages = page_table.shape[1]
    page, rank = pool_lat.shape[2], pool_lat.shape[3]
    rope = pool_rope.shape[3]
    pps = pages_per_step
    q3 = jnp.transpose(qh, (1, 0, 2))
    kself = kcat.reshape(n, 1, KCAT_W)
    pt_flat = page_table.reshape(-1)

    def page_map(j):
        return lambda b, s, pt: (layer, pt[b * n_pages + s * pps + j], 0, 0)

    grid_spec = pltpu.PrefetchScalarGridSpec(
        num_scalar_prefetch=1,
        grid=(n, n_pages // pps),
        in_specs=(
            [pl.BlockSpec((None, MLA_HEADS, KCAT_W), lambda b, s, pt: (b, 0, 0)),
             pl.BlockSpec((None, 1, KCAT_W), lambda b, s, pt: (b, 0, 0))]
            + [pl.BlockSpec((None, None, page, rank), page_map(j)) for j in range(pps)]
            + [pl.BlockSpec((None, None, page, rope), page_map(j)) for j in range(pps)]),
        out_specs=pl.BlockSpec((None, MLA_HEADS, MLA_KV_RANK), lambda b, s, pt: (b, 0, 0)),
        scratch_shapes=[pltpu.VMEM((MLA_HEADS, 1), F32), pltpu.VMEM((MLA_HEADS, 1), F32),
                        pltpu.VMEM((MLA_HEADS, MLA_KV_RANK), F32)],
    )
    return pl.pallas_call(
        functools.partial(_paged_kernel, pages_per_step=pps),
        grid_spec=grid_spec,
        out_shape=jax.ShapeDtypeStruct((n, MLA_HEADS, MLA_KV_RANK), F32),
        compiler_params=_params(2),
        name="attention_paged",
    )(pt_flat, q3, kself, *([pool_lat] * pps), *([pool_rope] * pps))


def _uv_kernel(o_ref, wuv_ref, y_ref):
    for p in range(MLA_HEADS // 2):
        pair = o_ref[:, 2 * p * MLA_KV_RANK:(2 * p + 2) * MLA_KV_RANK].astype(BF16)
        y_ref[:, p * LANE:(p + 1) * LANE] = _dot(pair, wuv_ref[p])


def _uv_step(o_lat, lw):
    n = o_lat.shape[0]
    o2 = o_lat.reshape(n, MLA_HEADS * MLA_KV_RANK)
    full = lambda shape: pl.BlockSpec(shape, lambda i: (0,) * len(shape))
    return pl.pallas_call(
        _uv_kernel,
        grid=(1,),
        in_specs=[full(o2.shape), full((MLA_HEADS // 2, 2 * MLA_KV_RANK, LANE))],
        out_specs=full((n, MLA_HEADS * MLA_V)),
        out_shape=jax.ShapeDtypeStruct((n, MLA_HEADS * MLA_V), F32),
        compiler_params=_params(1),
        name="uv_step",
    )(o2, lw["w_uv_pair"])


def _ffn_kernel(*refs, stepwise, final_norm, n_chunks):
    x_ref, yret_ref, ymla_ref, yrg_ref, wout_ref, n2_ref, wup_ref, cw_ref, cb_ref, wdn_ref = refs[:10]
    pos = 10
    fn_ref = None
    if final_norm:
        fn_ref = refs[pos]
        pos += 1
    if stepwise:
        bm2_ref, bm1_ref = refs[pos:pos + 2]
        out_ref, gate_ref = refs[pos + 2:pos + 4]
    else:
        out_ref, tail_ref, g_scr, carry_scr = refs[pos:pos + 4]
    tm = x_ref.shape[0]
    d_ff = wdn_ref.shape[0]
    tf = d_ff // n_chunks
    halo = 8

    if not stepwise:
        @pl.when(pl.program_id(1) == 0)
        def _():
            carry_scr[...] = jnp.zeros_like(carry_scr)

    mix = jnp.concatenate([yret_ref[...], ymla_ref[...], yrg_ref[...]], axis=1).astype(BF16)
    x1 = x_ref[...] + _dot(mix, wout_ref[...])
    h2 = _rms(x1, n2_ref[...]).astype(BF16)
    acc = jnp.zeros_like(x1)
    for c in range(n_chunks):
        lo, hi = c * tf, (c + 1) * tf
        gate = _dot(h2, wup_ref[:, lo:hi])
        val = _dot(h2, wup_ref[:, d_ff + lo:d_ff + hi])
        if stepwise:
            gm2 = bm2_ref[:, lo:hi]
            gm1 = bm1_ref[:, lo:hi]
            gate_ref[:, lo:hi] = gate
        else:
            g_scr[0:halo, :] = carry_scr[c]
            g_scr[halo:halo + tm, :] = gate
            gm1 = g_scr[halo - 1:halo - 1 + tm, :]
            gm2 = g_scr[halo - 2:halo - 2 + tm, :]
            carry_scr[c] = g_scr[tm:tm + halo, :]
            tail_ref[:, lo:hi] = gate[tm - (FFN_CONV - 1):tm, :]
        gc = (cb_ref[:, lo:hi] + gm2 * cw_ref[0:1, lo:hi] + gm1 * cw_ref[1:2, lo:hi]
              + gate * cw_ref[2:3, lo:hi])
        act = (gc * jax.nn.sigmoid(gc) * val).astype(BF16)
        acc = acc + _dot(act, wdn_ref[lo:hi, :])
    x2 = x1 + acc
    out_ref[...] = _rms(x2, fn_ref[...]) if final_norm else x2


def _outproj_ffn(x2d, y_ret, y_mla, y_rg, lw, n_batch, tm, final_w, step_buf=None):
    rows, d = x2d.shape
    d_ff = lw["ffn_w_down"].shape[0]
    n_chunks = 2
    tf = d_ff // n_chunks
    stepwise = step_buf is not None
    final_norm = final_w is not None
    nt = rows // (n_batch * tm)
    row_map = lambda b, t: (b * nt + t, 0)
    in_specs = [
        pl.BlockSpec((tm, d), row_map),
        pl.BlockSpec((tm, RET_W), row_map),
        pl.BlockSpec((tm, MLA_HEADS * MLA_V), row_map),
        pl.BlockSpec((tm, RG_WIDTH), row_map),
        _const_spec((d, d)), _const_spec((1, d)), _const_spec((d, 2 * d_ff)),
        _const_spec((FFN_CONV, d_ff)), _const_spec((1, d_ff)), _const_spec((d_ff, d)),
    ]
    args = [x2d, y_ret, y_mla, y_rg, lw["w_out"], lw["norm2_w"], lw["ffn_w_up"], lw["ffn_conv_w"],
            lw["ffn_conv_b"], lw["ffn_w_down"]]
    if final_norm:
        in_specs.append(_const_spec((1, d)))
        args.append(final_w)
    if stepwise:
        in_specs += [pl.BlockSpec((tm, d_ff), row_map)] * 2
        args += [step_buf[:, 0], step_buf[:, 1]]
        out_specs = [pl.BlockSpec((tm, d), row_map), pl.BlockSpec((tm, d_ff), row_map)]
        out_shape = (jax.ShapeDtypeStruct((rows, d), F32), jax.ShapeDtypeStruct((rows, d_ff), F32))
        scratch = []
    else:
        out_specs = [pl.BlockSpec((tm, d), row_map),
                     pl.BlockSpec((None, FFN_CONV - 1, d_ff), lambda b, t: (b, 0, 0))]
        out_shape = (jax.ShapeDtypeStruct((rows, d), F32),
                     jax.ShapeDtypeStruct((n_batch, FFN_CONV - 1, d_ff), F32))
        scratch = [pltpu.VMEM((tm + 8, tf), F32), pltpu.VMEM((n_chunks, 8, tf), F32)]
    return pl.pallas_call(
        functools.partial(_ffn_kernel, stepwise=stepwise, final_norm=final_norm, n_chunks=n_chunks),
        grid=(n_batch, nt),
        in_specs=in_specs,
        out_specs=out_specs,
        out_shape=out_shape,
        scratch_shapes=scratch,
        compiler_params=_params(2),
        name="outproj_ffn_step" if stepwise else "outproj_ffn",
    )(*args)


def _rope_tables(pos):
    def cos_sin(half):
        freqs = ROPE_BASE ** (-jnp.arange(half, dtype=F32) / half)
        ang = pos[:, None] * freqs[None, :]
        return jnp.cos(ang), jnp.sin(ang)

    def three(cos, sin, reps, width):
        zero = jnp.zeros_like(sin)
        tabs = [jnp.concatenate([cos, cos], 1), jnp.concatenate([-sin, zero], 1),
                jnp.concatenate([zero, sin], 1)]
        tabs = [jnp.tile(t, (1, reps)) for t in tabs]
        tabs = [jnp.pad(t, ((0, 0), (0, width - t.shape[1]))) for t in tabs]
        return jnp.stack(tabs)

    cos_r, sin_r = cos_sin(RET_DK // 2)
    cos_m, sin_m = cos_sin(MLA_ROPE // 2)
    return (three(cos_r, sin_r, RET_HEADS, RET_W), three(cos_m, sin_m, MLA_HEADS, QR_W),
            three(cos_m, sin_m, 1, LANE))


def _retention_tables(chunk):
    lg = jnp.log(1.0 - 2.0 ** (-5.0 - jnp.arange(RET_HEADS, dtype=F32)))
    idx = jnp.arange(chunk, dtype=F32)
    rel = idx[:, None] - idx[None, :]
    decay = jnp.where(rel >= 0, jnp.exp(lg[:, None, None] * jnp.maximum(rel, 0.0)), 0.0)
    per_lane = lambda t: jnp.repeat(t, RET_DK, axis=1)
    gq = per_lane(jnp.exp(lg[None, :] * (idx[:, None] + 1.0)))
    wk = per_lane(jnp.exp(lg[None, :] * (chunk - 1.0 - idx[:, None])))
    gl = jnp.broadcast_to(jnp.repeat(jnp.exp(lg * chunk), RET_DK)[:, None], (RET_W, RET_W))
    head = jnp.arange(RET_W) // RET_DK
    bd = (head[:, None] == head[None, :]).astype(F32)
    gam4 = jnp.broadcast_to(jnp.exp(lg * 1.0)[None, :, None, None], (1, RET_HEADS, RET_DK, RET_DV))
    return dict(chunk=chunk, decay=decay, gq=gq, wk=wk, gl=gl, bd=bd, avg=bd / RET_DV, gam4=gam4)


def _block_diag(w):
    n, c, dd = w.shape
    out = jnp.zeros((n * c, n * dd), w.dtype)
    for i in range(n):
        out = out.at[i * c:(i + 1) * c, i * dd:(i + 1) * dd].set(w[i])
    return out


def _layer_weights(l, P):
    w_in = P["w_in"][l]
    bounds = [0, 256, 512, 768, 1024, 1280, 1408, 1440, 1696, 1952]
    kr_lo, kr_hi = bounds[6], bounds[7]
    w_in_p = jnp.concatenate(
        [w_in[:, :kr_lo], w_in[:, kr_hi:], w_in[:, kr_lo:kr_hi],
         jnp.zeros((w_in.shape[0], _N_IN_PAD - w_in.shape[1]), w_in.dtype)], axis=1)
    w_uq = P["mla_w_uq"][l]
    w_uk = P["mla_w_uk"][l]
    w_uv = P["mla_w_uv"][l]
    uk_pair = jnp.stack([_block_diag(jnp.stack([w_uk[:, 2 * p, :].T, w_uk[:, 2 * p + 1, :].T]))
                         for p in range(MLA_HEADS // 2)])
    uv_pair = jnp.stack([_block_diag(jnp.stack([w_uv[:, 2 * p, :], w_uv[:, 2 * p + 1, :]]))
                         for p in range(MLA_HEADS // 2)])
    row = lambda v: v.reshape(1, -1)
    return dict(
        norm1_w=row(P["norm1_w"][l]), w_in=w_in_p.astype(BF16),
        q_norm_w=row(P["mla_q_norm_w"][l]), kv_norm_w=row(P["mla_kv_norm_w"][l]),
        w_uq_nope=w_uq[:, :, :MLA_NOPE].reshape(MLA_Q_RANK, QN_W).astype(BF16),
        w_uq_rope=w_uq[:, :, MLA_NOPE:].reshape(MLA_Q_RANK, QR_W).astype(BF16),
        w_uk_pair=uk_pair.astype(BF16), w_uv_pair=uv_pair.astype(BF16),
        ret_gn_w=row(P["ret_gn_w"][l]),
        rg_conv_w=P["rg_conv_w"][l], rg_conv_b=row(P["rg_conv_b"][l]),
        rg_wa_bd=_block_diag(P["rg_w_a"][l]).astype(BF16), rg_b_a=row(P["rg_b_a"][l]),
        rg_wx_bd=_block_diag(P["rg_w_x"][l]).astype(BF16), rg_b_x=row(P["rg_b_x"][l]),
        rg_lambda=row(P["rg_lambda"][l]),
        w_out=P["w_out"][l].astype(BF16), norm2_w=row(P["norm2_w"][l]),
        ffn_w_up=P["ffn_w_up"][l].astype(BF16), ffn_conv_w=P["ffn_conv_w"][l],
        ffn_conv_b=row(P["ffn_conv_b"][l]), ffn_w_down=P["ffn_w_down"][l].astype(BF16),
    )


def _tile(n, pref):
    t = min(n, pref)
    assert n % t == 0, (n, pref)
    return t


def kernel(x_prompt, x_sample, cache_mla_latent, cache_mla_krope, page_table, state_ret, state_rg_conv,
           state_rglru, state_ffn_conv, norm1_w, w_in, ret_gn_w, mla_q_norm_w, mla_w_uq, mla_kv_norm_w,
           mla_w_uk, mla_w_uv, rg_conv_w, rg_conv_b, rg_w_a, rg_b_a, rg_w_x, rg_b_x, rg_lambda, w_out,
           norm2_w, ffn_w_up, ffn_conv_w, ffn_conv_b, ffn_w_down, final_norm_w):
    P = dict(norm1_w=norm1_w, w_in=w_in, ret_gn_w=ret_gn_w, mla_q_norm_w=mla_q_norm_w, mla_w_uq=mla_w_uq,
             mla_kv_norm_w=mla_kv_norm_w, mla_w_uk=mla_w_uk, mla_w_uv=mla_w_uv, rg_conv_w=rg_conv_w,
             rg_conv_b=rg_conv_b, rg_w_a=rg_w_a, rg_b_a=rg_b_a, rg_w_x=rg_w_x, rg_b_x=rg_b_x,
             rg_lambda=rg_lambda, w_out=w_out, norm2_w=norm2_w, ffn_w_up=ffn_w_up, ffn_conv_w=ffn_conv_w,
             ffn_conv_b=ffn_conv_b, ffn_w_down=ffn_w_down)
    depth = w_in.shape[0]
    B, S, D = x_prompt.shape
    DB, T, _ = x_sample.shape
    assert T == 1, "the sample path handles one new token per sequence"
    n_pages, page = page_table.shape[1], cache_mla_latent.shape[2]
    past_len = n_pages * page
    final_w = final_norm_w.reshape(1, D)
    weights = [_layer_weights(l, P) for l in range(depth)]

    chunk = RET_CHUNK if (S > RET_CHUNK and S % RET_CHUNK == 0) else S
    rt = _retention_tables(chunk)
    tabs_p = _rope_tables(jnp.arange(S, dtype=F32))
    tm_a, tt, tq, tm_e = _tile(S, 512), _tile(S, 512), _tile(S, 256), _tile(S, 256)
    x = x_prompt.reshape(B * S, D)
    p_new = []
    for l in range(depth):
        lw = weights[l]
        ret_in, qh, kcat, p_lat, p_rope, rg_in = _inproj(x, B, tm_a, lw, tabs_p)
        y_ret, sbd = _retention_prompt(ret_in, B, lw, rt)
        y_rg, p_rgc, p_rgh = _rglru_prompt(rg_in, B, tt, lw)
        y_mla = _attention_prompt(qh, kcat, B, tq, lw)
        x, p_ffc = _outproj_ffn(x, y_ret, y_mla, y_rg, lw, B, tm_e,
                                final_w if l == depth - 1 else None)
        s_ret = jnp.stack([sbd[:, h * RET_DK:(h + 1) * RET_DK, h * RET_DV:(h + 1) * RET_DV]
                           for h in range(RET_HEADS)], axis=1)
        p_new.append((p_lat.reshape(B, S, -1), p_rope.reshape(B, S, -1), s_ret, p_rgc,
                      p_rgh.reshape(B, -1), p_ffc))
    y_prompt = x.reshape(B, S, D)

    tabs_s = _rope_tables(jnp.full((DB,), past_len, dtype=F32))
    pages_per_step = _tile(n_pages, 8)
    x = x_sample.reshape(DB, D)
    s_new = []
    for l in range(depth):
        lw = weights[l]
        ret_in, qh, kcat, s_lat, s_rope, rg_in = _inproj(x, 1, DB, lw, tabs_s)
        y_ret, s_ret = _retention_step(ret_in, state_ret[l], lw, rt)
        y_rg, s_rgh = _rglru_step(rg_in, state_rg_conv[l], state_rglru[l], lw)
        o_lat = _attention_paged(qh, kcat, cache_mla_latent, cache_mla_krope, page_table, l,
                                 pages_per_step)
        y_mla = _uv_step(o_lat, lw)
        x, gate = _outproj_ffn(x, y_ret, y_mla, y_rg, lw, 1, DB,
                               final_w if l == depth - 1 else None, step_buf=state_ffn_conv[l])
        s_rgc = jnp.concatenate([state_rg_conv[l][:, 1:], rg_in[:, None, 0:RG_WIDTH]], axis=1)
        s_ffc = jnp.concatenate([state_ffn_conv[l][:, 1:], gate[:, None, :]], axis=1)
        s_new.append((s_lat.reshape(DB, T, -1), s_rope.reshape(DB, T, -1), s_ret, s_rgc, s_rgh, s_ffc))
    y_sample = x.reshape(DB, T, D)

    stack = lambda items, i: jnp.stack([it[i] for it in items])
    return (y_prompt, y_sample, *[stack(p_new, i) for i in range(6)], *[stack(s_new, i) for i in range(6)])
```

```python
import functools

import jax
import jax.numpy as jnp
from jax import lax
from jax.experimental import pallas as pl
from jax.experimental.pallas import tpu as pltpu

RET_HEADS = 4
RET_DK = 64
RET_DV = 64
RET_CHUNK = 128
MLA_HEADS = 8
MLA_NOPE = 64
MLA_ROPE = 32
MLA_V = 64
MLA_Q_RANK = 256
MLA_KV_RANK = 128
MLA_SCALE = (MLA_NOPE + MLA_ROPE) ** -0.5
LOG2_E = 1.4426950408889634
Q_SCALE = MLA_SCALE * LOG2_E
RG_WIDTH = 256
RG_BLOCKS = 4
RG_CONV = 4
RG_C = 8.0
FFN_CONV = 3
ROPE_BASE = 10000.0
EPS = 1e-6
NEG_INF = -1e30

RET_W = RET_HEADS * RET_DK
QR_W = MLA_HEADS * MLA_ROPE
QN_W = MLA_HEADS * MLA_NOPE
KCAT_W = 256
LANE = 128
VMEM_LIMIT = 56 * 1024 * 1024

BF16 = jnp.bfloat16
F32 = jnp.float32

_OFF_RQ, _OFF_RK, _OFF_RV, _OFF_RG = 0, 256, 512, 768
_OFF_CQ, _OFF_CKV, _OFF_GX, _OFF_GG, _OFF_KR = 1024, 1280, 1408, 1664, 1920
_N_IN_PAD = 2048


def _params(n_axes):
    return pltpu.CompilerParams(dimension_semantics=("arbitrary",) * n_axes,
                                vmem_limit_bytes=VMEM_LIMIT)


def _const_spec(shape):
    n = len(shape)
    return pl.BlockSpec(shape, lambda *_: (0,) * n, pipeline_mode=pl.Buffered(1))


def _dot(a, b):
    return jnp.dot(a, b, preferred_element_type=F32)


def _dot_nt(a, b):
    return lax.dot_general(a, b, (((1,), (1,)), ((), ())), preferred_element_type=F32)


def _dot_tn(a, b):
    return lax.dot_general(a, b, (((0,), (0,)), ((), ())), preferred_element_type=F32)


def _rms(x, w):
    return x * lax.rsqrt(jnp.mean(x * x, axis=-1, keepdims=True) + EPS) * w


def _rope3(x, tab_ref, half):
    w = x.shape[-1]
    return (x * tab_ref[0] + pltpu.roll(x, w - half, 1) * tab_ref[1]
            + pltpu.roll(x, half, 1) * tab_ref[2])


def _inproj_kernel(x_ref, n1_ref, win_ref, tabr_ref, tabq_ref, tabk_ref, qnw_ref, wuqn_ref,
                   wuqr_ref, wukp_ref, kvnw_ref,
                   ret_ref, q_ref, kcat_ref, plat_ref, prope_ref, rg_ref):
    h = _rms(x_ref[...], n1_ref[...]).astype(BF16)
    z = _dot(h, win_ref[...])

    ret_ref[:, _OFF_RQ:_OFF_RQ + RET_W] = _rope3(z[:, _OFF_RQ:_OFF_RQ + RET_W], tabr_ref, RET_DK // 2)
    ret_ref[:, _OFF_RK:_OFF_RK + RET_W] = (
        _rope3(z[:, _OFF_RK:_OFF_RK + RET_W], tabr_ref, RET_DK // 2) * (RET_DK ** -0.5))
    ret_ref[:, _OFF_RV:_OFF_CQ] = z[:, _OFF_RV:_OFF_CQ]
    rg_ref[...] = z[:, _OFF_GX:_OFF_KR]

    ckvn = _rms(z[:, _OFF_CKV:_OFF_CKV + MLA_KV_RANK], kvnw_ref[...])
    plat_ref[...] = ckvn
    krr = _rope3(z[:, _OFF_KR:_OFF_KR + LANE], tabk_ref, MLA_ROPE // 2)
    prope_ref[...] = krr[:, :MLA_ROPE]
    kcat_ref[:, 0:MLA_KV_RANK] = ckvn.astype(BF16)
    kcat_ref[:, MLA_KV_RANK:KCAT_W] = krr.astype(BF16)

    cqn = _rms(z[:, _OFF_CQ:_OFF_CQ + MLA_Q_RANK], qnw_ref[...]).astype(BF16)
    qn = _dot(cqn, wuqn_ref[...]).astype(BF16)
    qr = _rope3(_dot(cqn, wuqr_ref[...]), tabq_ref, MLA_ROPE // 2) * Q_SCALE
    lane = lax.broadcasted_iota(jnp.int32, (1, LANE), 1)
    heads_per_tile = LANE // MLA_ROPE
    for p in range(MLA_HEADS // 2):
        ql = _dot(qn[:, p * LANE:(p + 1) * LANE], wukp_ref[p]) * Q_SCALE
        for e in range(2):
            hd = 2 * p + e
            q_ref[hd, :, 0:MLA_KV_RANK] = ql[:, e * LANE:(e + 1) * LANE].astype(BF16)
            tile = qr[:, (hd // heads_per_tile) * LANE:(hd // heads_per_tile + 1) * LANE]
            sh = MLA_ROPE * (hd % heads_per_tile)
            if sh:
                tile = pltpu.roll(tile, LANE - sh, 1)
            q_ref[hd, :, MLA_KV_RANK:KCAT_W] = jnp.where(lane < MLA_ROPE, tile, 0.0).astype(BF16)


def _inproj(x2d, n_batch, tm, lw, tabs):
    rows, d = x2d.shape
    ns = rows // (n_batch * tm)
    tabr, tabq, tabk = tabs
    row_map = lambda s, b: (b * ns + s, 0)
    tab_map = lambda s, b: (0, s, 0)
    out_shape = (
        jax.ShapeDtypeStruct((rows, 4 * RET_W), F32),
        jax.ShapeDtypeStruct((MLA_HEADS, rows, KCAT_W), BF16),
        jax.ShapeDtypeStruct((rows, KCAT_W), BF16),
        jax.ShapeDtypeStruct((rows, MLA_KV_RANK), F32),
        jax.ShapeDtypeStruct((rows, MLA_ROPE), F32),
        jax.ShapeDtypeStruct((rows, 2 * RG_WIDTH), F32),
    )
    return pl.pallas_call(
        _inproj_kernel,
        grid=(ns, n_batch),
        in_specs=[
            pl.BlockSpec((tm, d), row_map),
            _const_spec((1, d)),
            _const_spec((d, _N_IN_PAD)),
            pl.BlockSpec((3, tm, RET_W), tab_map),
            pl.BlockSpec((3, tm, QR_W), tab_map),
            pl.BlockSpec((3, tm, LANE), tab_map),
            _const_spec((1, MLA_Q_RANK)),
            _const_spec((MLA_Q_RANK, QN_W)),
            _const_spec((MLA_Q_RANK, QR_W)),
            _const_spec((MLA_HEADS // 2, LANE, 2 * MLA_KV_RANK)),
            _const_spec((1, MLA_KV_RANK)),
        ],
        out_specs=[
            pl.BlockSpec((tm, 4 * RET_W), row_map),
            pl.BlockSpec((MLA_HEADS, tm, KCAT_W), lambda s, b: (0, b * ns + s, 0)),
            pl.BlockSpec((tm, KCAT_W), row_map),
            pl.BlockSpec((tm, MLA_KV_RANK), row_map),
            pl.BlockSpec((tm, MLA_ROPE), row_map),
            pl.BlockSpec((tm, 2 * RG_WIDTH), row_map),
        ],
        out_shape=out_shape,
        compiler_params=_params(2),
        name="inproj",
    )(x2d, lw["norm1_w"], lw["w_in"], tabr, tabq, tabk, lw["q_norm_w"], lw["w_uq_nope"],
      lw["w_uq_rope"], lw["w_uk_pair"], lw["kv_norm_w"])


def _group_norm_gate(o, g, gnw, avg):
    mu = _dot(o, avg)
    d = o - mu
    var = _dot(d * d, avg)
    return d * lax.rsqrt(var + EPS) * gnw * (g * jax.nn.sigmoid(g))


def _ret_kernel(ret_ref, dec_ref, gq_ref, wk_ref, gl_ref, bd_ref, gnw_ref, avg_ref,
                y_ref, sbd_ref, s_scr):
    @pl.when(pl.program_id(1) == 0)
    def _():
        s_scr[...] = jnp.zeros_like(s_scr)

    q = ret_ref[:, _OFF_RQ:_OFF_RQ + RET_W]
    k = ret_ref[:, _OFF_RK:_OFF_RK + RET_W]
    vb = ret_ref[:, _OFF_RV:_OFF_RV + RET_W].astype(BF16)
    g = ret_ref[:, _OFF_RG:_OFF_RG + RET_W]
    kb = k.astype(BF16)
    s_old = s_scr[...]
    head_of_lane = lax.broadcasted_iota(jnp.int32, (1, RET_W), 1) // RET_DK
    o = _dot(q.astype(BF16), s_old.astype(BF16)) * gq_ref[...]
    for h in range(RET_HEADS):
        mh = head_of_lane == h
        s = _dot_nt(jnp.where(mh, q, 0.0).astype(BF16), kb) * dec_ref[h]
        o = o + jnp.where(mh, _dot(s.astype(BF16), vb), 0.0)
    upd = _dot_tn((k * wk_ref[...]).astype(BF16), vb)
    s_new = gl_ref[...] * s_old + bd_ref[...] * upd
    s_scr[...] = s_new
    sbd_ref[...] = s_new
    y_ref[...] = _group_norm_gate(o, g, gnw_ref[...], avg_ref[...])


def _retention_prompt(ret_in, n_batch, lw, rt):
    rows = ret_in.shape[0]
    L = rt["chunk"]
    nc = rows // (n_batch * L)
    return pl.pallas_call(
        _ret_kernel,
        grid=(n_batch, nc),
        in_specs=[
            pl.BlockSpec((L, 4 * RET_W), lambda b, c: (b * nc + c, 0)),
            _const_spec((RET_HEADS, L, L)),
            _const_spec((L, RET_W)),
            _const_spec((L, RET_W)),
            _const_spec((RET_W, RET_W)),
            _const_spec((RET_W, RET_W)),
            _const_spec((1, RET_W)),
            _const_spec((RET_W, RET_W)),
        ],
        out_specs=[
            pl.BlockSpec((L, RET_W), lambda b, c: (b * nc + c, 0)),
            pl.BlockSpec((None, RET_W, RET_W), lambda b, c: (b, 0, 0)),
        ],
        out_shape=(jax.ShapeDtypeStruct((rows, RET_W), F32),
                   jax.ShapeDtypeStruct((n_batch, RET_W, RET_W), F32)),
        scratch_shapes=[pltpu.VMEM((RET_W, RET_W), F32)],
        compiler_params=_params(2),
        name="retention",
    )(ret_in, rt["decay"], rt["gq"], rt["wk"], rt["gl"], rt["bd"], lw["ret_gn_w"], rt["avg"])


def _ret_step_kernel(q_ref, k_ref, v_ref, g_ref, s0_ref, gam_ref, gnw_ref, y_ref, s_ref):
    s_new = gam_ref[...] * s0_ref[...] + k_ref[...] * v_ref[...]
    s_ref[...] = s_new
    o = jnp.sum(q_ref[...] * s_new, axis=2, keepdims=True)
    mu = jnp.mean(o, axis=3, keepdims=True)
    d = o - mu
    var = jnp.mean(d * d, axis=3, keepdims=True)
    g = g_ref[...]
    y_ref[...] = d * lax.rsqrt(var + EPS) * gnw_ref[...] * (g * jax.nn.sigmoid(g))


def _retention_step(ret_in, s0, lw, rt):
    n = ret_in.shape[0]
    bs = 8
    q4 = ret_in[:, _OFF_RQ:_OFF_RQ + RET_W].reshape(n, RET_HEADS, RET_DK, 1)
    k4 = ret_in[:, _OFF_RK:_OFF_RK + RET_W].reshape(n, RET_HEADS, RET_DK, 1)
    v4 = ret_in[:, _OFF_RV:_OFF_RV + RET_W].reshape(n, RET_HEADS, 1, RET_DV)
    g4 = ret_in[:, _OFF_RG:_OFF_RG + RET_W].reshape(n, RET_HEADS, 1, RET_DV)
    col = pl.BlockSpec((bs, RET_HEADS, RET_DK, 1), lambda i: (i, 0, 0, 0))
    row = pl.BlockSpec((bs, RET_HEADS, 1, RET_DV), lambda i: (i, 0, 0, 0))
    mat = pl.BlockSpec((bs, RET_HEADS, RET_DK, RET_DV), lambda i: (i, 0, 0, 0))
    y4, s_new = pl.pallas_call(
        _ret_step_kernel,
        grid=(n // bs,),
        in_specs=[col, col, row, row, mat,
                  _const_spec((1, RET_HEADS, RET_DK, RET_DV)),
                  _const_spec((1, RET_HEADS, 1, RET_DV))],
        out_specs=[row, mat],
        out_shape=(jax.ShapeDtypeStruct((n, RET_HEADS, 1, RET_DV), F32),
                   jax.ShapeDtypeStruct((n, RET_HEADS, RET_DK, RET_DV), F32)),
        compiler_params=_params(1),
        name="retention_step",
    )(q4, k4, v4, g4, s0, rt["gam4"], lw["ret_gn_w"].reshape(1, RET_HEADS, 1, RET_DV))
    return y4.reshape(n, RET_W), s_new


def _rg_gates(xc, wa_ref, ba_ref, wx_ref, bx_ref, lam_ref):
    xb = xc.astype(BF16)
    r = jax.nn.sigmoid(_dot(xb, wa_ref[...]) + ba_ref[...])
    i = jax.nn.sigmoid(_dot(xb, wx_ref[...]) + bx_ref[...])
    y = -lam_ref[...]
    softplus = jnp.maximum(y, 0.0) + jnp.log1p(jnp.exp(-jnp.abs(y)))
    log_a = -RG_C * r * softplus
    a = jnp.exp(log_a)
    u = jnp.sqrt(1.0 - jnp.exp(2.0 * log_a)) * i * xc
    return a, u


def _rg_kernel(rg_ref, cw_ref, cb_ref, wa_ref, ba_ref, wx_ref, bx_ref, lam_ref,
               y_ref, buf_ref, hlast_ref, xs_scr, h_scr):
    tt = rg_ref.shape[0]
    halo = 8

    @pl.when(pl.program_id(1) == 0)
    def _():
        xs_scr[0:halo, :] = jnp.zeros((halo, RG_WIDTH), F32)
        h_scr[...] = jnp.zeros_like(h_scr)

    gx = rg_ref[:, 0:RG_WIDTH]
    gg = rg_ref[:, RG_WIDTH:2 * RG_WIDTH]
    xs_scr[halo:halo + tt, :] = gx
    xc = cb_ref[...] + gx * cw_ref[RG_CONV - 1:RG_CONV, :]
    for j in range(1, RG_CONV):
        xc = xc + xs_scr[halo - j:halo - j + tt, :] * cw_ref[RG_CONV - 1 - j:RG_CONV - j, :]
    xs_scr[0:halo, :] = xs_scr[tt:tt + halo, :]
    buf_ref[...] = gx[tt - (RG_CONV - 1):tt, :]

    a, u = _rg_gates(xc, wa_ref, ba_ref, wx_ref, bx_ref, lam_ref)
    row = lax.broadcasted_iota(jnp.int32, (tt, 1), 0)
    step = 1
    while step < tt:
        keep = row >= step
        a_sh = jnp.where(keep, pltpu.roll(a, step, 0), 1.0)
        u_sh = jnp.where(keep, pltpu.roll(u, step, 0), 0.0)
        u = a * u_sh + u
        a = a * a_sh
        step *= 2
    h = u + a * h_scr[...]
    h_scr[...] = h[tt - 1:tt, :]
    hlast_ref[...] = h[tt - 1:tt, :]
    y_ref[...] = h * jax.nn.gelu(gg, approximate=True)


def _rglru_prompt(rg_in, n_batch, tt, lw):
    rows = rg_in.shape[0]
    nt = rows // (n_batch * tt)
    w = RG_WIDTH
    return pl.pallas_call(
        _rg_kernel,
        grid=(n_batch, nt),
        in_specs=[
            pl.BlockSpec((tt, 2 * w), lambda b, t: (b * nt + t, 0)),
            _const_spec((RG_CONV, w)), _const_spec((1, w)),
            _const_spec((w, w)), _const_spec((1, w)),
            _const_spec((w, w)), _const_spec((1, w)), _const_spec((1, w)),
        ],
        out_specs=[
            pl.BlockSpec((tt, w), lambda b, t: (b * nt + t, 0)),
            pl.BlockSpec((None, RG_CONV - 1, w), lambda b, t: (b, 0, 0)),
            pl.BlockSpec((None, 1, w), lambda b, t: (b, 0, 0)),
        ],
        out_shape=(jax.ShapeDtypeStruct((rows, w), F32),
                   jax.ShapeDtypeStruct((n_batch, RG_CONV - 1, w), F32),
                   jax.ShapeDtypeStruct((n_batch, 1, w), F32)),
        scratch_shapes=[pltpu.VMEM((tt + 8, w), F32), pltpu.VMEM((1, w), F32)],
        compiler_params=_params(2),
        name="rglru",
    )(rg_in, lw["rg_conv_w"], lw["rg_conv_b"], lw["rg_wa_bd"], lw["rg_b_a"], lw["rg_wx_bd"],
      lw["rg_b_x"], lw["rg_lambda"])


def _rg_step_kernel(rg_ref, b0_ref, b1_ref, b2_ref, h0_ref, cw_ref, cb_ref, wa_ref, ba_ref,
                    wx_ref, bx_ref, lam_ref, y_ref, h_ref):
    gx = rg_ref[:, 0:RG_WIDTH]
    gg = rg_ref[:, RG_WIDTH:2 * RG_WIDTH]
    xc = (cb_ref[...] + b0_ref[...] * cw_ref[0:1, :] + b1_ref[...] * cw_ref[1:2, :]
          + b2_ref[...] * cw_ref[2:3, :] + gx * cw_ref[3:4, :])
    a, u = _rg_gates(xc, wa_ref, ba_ref, wx_ref, bx_ref, lam_ref)
    h = u + a * h0_ref[...]
    h_ref[...] = h
    y_ref[...] = h * jax.nn.gelu(gg, approximate=True)


def _rglru_step(rg_in, buf0, h0, lw):
    n = rg_in.shape[0]
    w = RG_WIDTH
    full = lambda shape: pl.BlockSpec(shape, lambda i: (0,) * len(shape))
    return pl.pallas_call(
        _rg_step_kernel,
        grid=(1,),
        in_specs=[full((n, 2 * w)), full((n, w)), full((n, w)), full((n, w)), full((n, w)),
                  full((RG_CONV, w)), full((1, w)), full((w, w)), full((1, w)), full((w, w)),
                  full((1, w)), full((1, w))],
        out_specs=[full((n, w)), full((n, w))],
        out_shape=(jax.ShapeDtypeStruct((n, w), F32), jax.ShapeDtypeStruct((n, w), F32)),
        compiler_params=_params(1),
        name="rglru_step",
    )(rg_in, buf0[:, 0], buf0[:, 1], buf0[:, 2], h0, lw["rg_conv_w"], lw["rg_conv_b"],
      lw["rg_wa_bd"], lw["rg_b_a"], lw["rg_wx_bd"], lw["rg_b_x"], lw["rg_lambda"])


def _uv_project(o, tq, wuv_ref, out_ref):
    for p in range(MLA_HEADS // 2):
        pair = jnp.concatenate([o[(2 * p) * tq:(2 * p + 1) * tq], o[(2 * p + 1) * tq:(2 * p + 2) * tq]],
                               axis=1).astype(BF16)
        out_ref[:, p * LANE:(p + 1) * LANE] = _dot(pair, wuv_ref[p])


def _attn_kernel(q_ref, k_ref, wuv_ref, o_ref, m_scr, l_scr, acc_scr, *, kb):
    tq = q_ref.shape[1]
    m_rows = MLA_HEADS * tq
    i = pl.program_id(1)
    q = q_ref[...].reshape(m_rows, KCAT_W)
    m_scr[...] = jnp.full_like(m_scr, NEG_INF)
    l_scr[...] = jnp.zeros_like(l_scr)
    acc_scr[...] = jnp.zeros_like(acc_scr)

    def block(start, width, masked):
        kblk = k_ref[pl.ds(pl.multiple_of(start, tq), width), :]
        s = _dot_nt(q, kblk)
        if masked:
            qpos = lax.broadcasted_iota(jnp.int32, (MLA_HEADS, tq, width), 1).reshape(m_rows, width)
            kpos = lax.broadcasted_iota(jnp.int32, (m_rows, width), 1)
            s = jnp.where(kpos <= qpos, s, NEG_INF)
        m_prev = m_scr[...]
        m_next = jnp.maximum(m_prev, jnp.max(s, axis=1, keepdims=True))
        p = jnp.exp2(s - jnp.concatenate([m_next] * (width // LANE), axis=1))
        alpha = jnp.exp2(m_prev - m_next)
        l_scr[...] = alpha * l_scr[...] + jnp.sum(p, axis=1, keepdims=True)
        acc_scr[...] = alpha * acc_scr[...] + _dot(p.astype(BF16), kblk[:, 0:MLA_KV_RANK])
        m_scr[...] = m_next

    n_wide = (i * tq) // kb
    n_narrow = (i * tq - n_wide * kb) // tq

    def wide(j, carry):
        block(j * kb, kb, False)
        return carry

    def narrow(j, carry):
        block(n_wide * kb + j * tq, tq, False)
        return carry

    lax.fori_loop(0, n_wide, wide, 0)
    lax.fori_loop(0, n_narrow, narrow, 0)
    block(i * tq, tq, True)
    o = acc_scr[...] / l_scr[...]
    _uv_project(o, tq, wuv_ref, o_ref)


def _attention_prompt(qh, kcat, n_batch, tq, kb, lw):
    rows = kcat.shape[0]
    s_len = rows // n_batch
    nq = s_len // tq
    m_rows = MLA_HEADS * tq
    return pl.pallas_call(
        functools.partial(_attn_kernel, kb=kb),
        grid=(n_batch, nq),
        in_specs=[
            pl.BlockSpec((MLA_HEADS, tq, KCAT_W), lambda b, i: (0, b * nq + i, 0)),
            pl.BlockSpec((s_len, KCAT_W), lambda b, i: (b, 0)),
            _const_spec((MLA_HEADS // 2, 2 * MLA_KV_RANK, LANE)),
        ],
        out_specs=pl.BlockSpec((tq, MLA_HEADS * MLA_V), lambda b, i: (b * nq + i, 0)),
        out_shape=jax.ShapeDtypeStruct((rows, MLA_HEADS * MLA_V), F32),
        scratch_shapes=[pltpu.VMEM((m_rows, LANE), F32), pltpu.VMEM((m_rows, LANE), F32),
                        pltpu.VMEM((m_rows, MLA_KV_RANK), F32)],
        compiler_params=_params(2),
        name="attention",
    )(qh, kcat, lw["w_uv_pair"])


def _paged_kernel(pt_ref, q_ref, kself_ref, lat_hbm, krt_hbm, o_ref,
                  lat_buf, krt_buf, sem, m_scr, l_scr, acc_scr, *, layer, pages_per_chunk,
                  chunks_per_seq, sub_keys):
    g = pl.program_id(0)
    page = lat_hbm.shape[2]
    chunk_keys = pages_per_chunk * page

    last = pl.num_programs(0) - 1
    slot = g % 2
    other = 1 - slot
    n_sub = chunk_keys // sub_keys
    pages_per_sub = pages_per_chunk // n_sub

    def page_copies(pid, dst_slot, j):
        cols = pl.ds(j * page, page)
        return (pltpu.make_async_copy(lat_hbm.at[layer, pid], lat_buf.at[dst_slot, cols, :], sem.at[dst_slot, 0]),
                pltpu.make_async_copy(krt_hbm.at[layer, pid], krt_buf.at[dst_slot, :, cols], sem.at[dst_slot, 1]))

    def start_pages(step, dst_slot, pages):
        for j in pages:
            for cp in page_copies(pt_ref[step * pages_per_chunk + j], dst_slot, j):
                cp.start()

    def wait_chunk(dst_slot):
        for j in range(pages_per_chunk):
            for cp in page_copies(0, dst_slot, j):
                cp.wait()

    @pl.when(g == 0)
    def _():
        start_pages(0, 0, range(pages_per_chunk))

    wait_chunk(slot)
    c = g % chunks_per_seq

    @pl.when(c == 0)
    def _():
        m_scr[...] = jnp.full_like(m_scr, NEG_INF)
        l_scr[...] = jnp.zeros_like(l_scr)
        acc_scr[...] = jnp.zeros_like(acc_scr)

    q = q_ref[...]
    q_lat = q[:, 0:MLA_KV_RANK]
    q_rope = q[:, MLA_KV_RANK:MLA_KV_RANK + MLA_ROPE]
    nxt = jnp.minimum(g + 1, last)
    lats, scores = [], []
    for sb in range(n_sub):
        start_pages(nxt, other, range(sb * pages_per_sub, (sb + 1) * pages_per_sub))
        keys = pl.ds(sb * sub_keys, sub_keys)
        lat = lat_buf[slot, keys, :].astype(BF16)
        krt = krt_buf[slot, :, keys].astype(BF16)
        lats.append(lat)
        scores.append(_dot_nt(q_lat, lat) + _dot(q_rope, krt))
    m_prev = m_scr[...]
    m_cur = functools.reduce(jnp.maximum, [jnp.max(s, axis=1, keepdims=True) for s in scores])
    m_next = jnp.maximum(m_prev, m_cur)
    alpha = jnp.exp2(m_prev - m_next)
    l_new = alpha * l_scr[...]
    acc = alpha * acc_scr[...]
    for s, lat in zip(scores, lats):
        p = jnp.exp2(s - m_next)
        l_new = l_new + jnp.sum(p, axis=1, keepdims=True)
        acc = acc + _dot(p.astype(BF16), lat)
    m_scr[...] = m_next
    l_scr[...] = l_new
    acc_scr[...] = acc

    @pl.when(g == last)
    def _():
        wait_chunk(other)

    @pl.when(c == chunks_per_seq - 1)
    def _():
        kself = kself_ref[...].astype(F32)
        s_self = jnp.sum(q.astype(F32) * kself, axis=1, keepdims=True)
        m_prev = m_scr[...]
        m_fin = jnp.maximum(m_prev, s_self)
        alpha = jnp.exp2(m_prev - m_fin)
        p_self = jnp.exp2(s_self - m_fin)
        l_fin = alpha * l_scr[...] + p_self
        acc_fin = alpha * acc_scr[...] + p_self * kself[:, 0:MLA_KV_RANK]
        o_ref[...] = acc_fin / l_fin


def _attention_paged(qh, kcat, pool_lat, pool_rope_t, page_table, layer, pages_per_chunk):
    n = kcat.shape[0]
    n_pages = page_table.shape[1]
    page, rank = pool_lat.shape[2], pool_lat.shape[3]
    rope = pool_rope_t.shape[2]
    ppc = pages_per_chunk
    cps = n_pages // ppc
    chunk_keys = ppc * page
    q3 = jnp.transpose(qh, (1, 0, 2))
    kself = kcat.reshape(n, 1, KCAT_W)
    pt_flat = page_table.reshape(-1)
    seq_map = lambda g, pt: (g // cps, 0, 0)
    grid_spec = pltpu.PrefetchScalarGridSpec(
        num_scalar_prefetch=1,
        grid=(n * cps,),
        in_specs=[pl.BlockSpec((None, MLA_HEADS, KCAT_W), seq_map),
                  pl.BlockSpec((None, 1, KCAT_W), seq_map),
                  pl.BlockSpec(memory_space=pl.ANY),
                  pl.BlockSpec(memory_space=pl.ANY)],
        out_specs=pl.BlockSpec((None, MLA_HEADS, MLA_KV_RANK), seq_map),
        scratch_shapes=[pltpu.VMEM((2, chunk_keys, rank), F32),
                        pltpu.VMEM((2, rope, chunk_keys), F32),
                        pltpu.SemaphoreType.DMA((2, 2)),
                        pltpu.VMEM((MLA_HEADS, 1), F32), pltpu.VMEM((MLA_HEADS, 1), F32),
                        pltpu.VMEM((MLA_HEADS, MLA_KV_RANK), F32)],
    )
    return pl.pallas_call(
        functools.partial(_paged_kernel, layer=layer, pages_per_chunk=ppc, chunks_per_seq=cps,
                          sub_keys=min(chunk_keys, 2048)),
        grid_spec=grid_spec,
        out_shape=jax.ShapeDtypeStruct((n, MLA_HEADS, MLA_KV_RANK), F32),
        compiler_params=_params(1),
        name="attention_paged",
    )(pt_flat, q3, kself, pool_lat, pool_rope_t)


def _uv_kernel(o_ref, wuv_ref, y_ref):
    for p in range(MLA_HEADS // 2):
        pair = o_ref[:, 2 * p * MLA_KV_RANK:(2 * p + 2) * MLA_KV_RANK].astype(BF16)
        y_ref[:, p * LANE:(p + 1) * LANE] = _dot(pair, wuv_ref[p])


def _uv_step(o_lat, lw):
    n = o_lat.shape[0]
    o2 = o_lat.reshape(n, MLA_HEADS * MLA_KV_RANK)
    full = lambda shape: pl.BlockSpec(shape, lambda i: (0,) * len(shape))
    return pl.pallas_call(
        _uv_kernel,
        grid=(1,),
        in_specs=[full(o2.shape), full((MLA_HEADS // 2, 2 * MLA_KV_RANK, LANE))],
        out_specs=full((n, MLA_HEADS * MLA_V)),
        out_shape=jax.ShapeDtypeStruct((n, MLA_HEADS * MLA_V), F32),
        compiler_params=_params(1),
        name="uv_step",
    )(o2, lw["w_uv_pair"])


def _ffn_kernel(*refs, stepwise, final_norm, n_chunks):
    x_ref, yret_ref, ymla_ref, yrg_ref, wout_ref, n2_ref, wup_ref, cw_ref, cb_ref, wdn_ref = refs[:10]
    pos = 10
    fn_ref = None
    if final_norm:
        fn_ref = refs[pos]
        pos += 1
    if stepwise:
        bm2_ref, bm1_ref = refs[pos:pos + 2]
        out_ref, gate_ref = refs[pos + 2:pos + 4]
    else:
        out_ref, tail_ref, g_scr, carry_scr = refs[pos:pos + 4]
    tm = x_ref.shape[0]
    d_ff = wdn_ref.shape[0]
    tf = d_ff // n_chunks
    halo = 8

    if not stepwise:
        @pl.when(pl.program_id(1) == 0)
        def _():
            carry_scr[...] = jnp.zeros_like(carry_scr)

    mix = jnp.concatenate([yret_ref[...], ymla_ref[...], yrg_ref[...]], axis=1).astype(BF16)
    x1 = x_ref[...] + _dot(mix, wout_ref[...])
    h2 = _rms(x1, n2_ref[...]).astype(BF16)
    acc = jnp.zeros_like(x1)
    for c in range(n_chunks):
        lo, hi = c * tf, (c + 1) * tf
        gate = _dot(h2, wup_ref[:, lo:hi])
        val = _dot(h2, wup_ref[:, d_ff + lo:d_ff + hi])
        if stepwise:
            gm2 = bm2_ref[:, lo:hi]
            gm1 = bm1_ref[:, lo:hi]
            gate_ref[:, lo:hi] = gate
        else:
            g_scr[0:halo, :] = carry_scr[c]
            g_scr[halo:halo + tm, :] = gate
            gm1 = g_scr[halo - 1:halo - 1 + tm, :]
            gm2 = g_scr[halo - 2:halo - 2 + tm, :]
            carry_scr[c] = g_scr[tm:tm + halo, :]
            tail_ref[:, lo:hi] = gate[tm - (FFN_CONV - 1):tm, :]
        gc = (cb_ref[:, lo:hi] + gm2 * cw_ref[0:1, lo:hi] + gm1 * cw_ref[1:2, lo:hi]
              + gate * cw_ref[2:3, lo:hi])
        act = (gc * jax.nn.sigmoid(gc) * val).astype(BF16)
        acc = acc + _dot(act, wdn_ref[lo:hi, :])
    x2 = x1 + acc
    out_ref[...] = _rms(x2, fn_ref[...]) if final_norm else x2


def _outproj_ffn(x2d, y_ret, y_mla, y_rg, lw, n_batch, tm, final_w, step_buf=None):
    rows, d = x2d.shape
    d_ff = lw["ffn_w_down"].shape[0]
    n_chunks = 2
    tf = d_ff // n_chunks
    stepwise = step_buf is not None
    final_norm = final_w is not None
    nt = rows // (n_batch * tm)
    row_map = lambda b, t: (b * nt + t, 0)
    in_specs = [
        pl.BlockSpec((tm, d), row_map),
        pl.BlockSpec((tm, RET_W), row_map),
        pl.BlockSpec((tm, MLA_HEADS * MLA_V), row_map),
        pl.BlockSpec((tm, RG_WIDTH), row_map),
        _const_spec((d, d)), _const_spec((1, d)), _const_spec((d, 2 * d_ff)),
        _const_spec((FFN_CONV, d_ff)), _const_spec((1, d_ff)), _const_spec((d_ff, d)),
    ]
    args = [x2d, y_ret, y_mla, y_rg, lw["w_out"], lw["norm2_w"], lw["ffn_w_up"], lw["ffn_conv_w"],
            lw["ffn_conv_b"], lw["ffn_w_down"]]
    if final_norm:
        in_specs.append(_const_spec((1, d)))
        args.append(final_w)
    if stepwise:
        in_specs += [pl.BlockSpec((tm, d_ff), row_map)] * 2
        args += [step_buf[:, 0], step_buf[:, 1]]
        out_specs = [pl.BlockSpec((tm, d), row_map), pl.BlockSpec((tm, d_ff), row_map)]
        out_shape = (jax.ShapeDtypeStruct((rows, d), F32), jax.ShapeDtypeStruct((rows, d_ff), F32))
        scratch = []
    else:
        out_specs = [pl.BlockSpec((tm, d), row_map),
                     pl.BlockSpec((None, FFN_CONV - 1, d_ff), lambda b, t: (b, 0, 0))]
        out_shape = (jax.ShapeDtypeStruct((rows, d), F32),
                     jax.ShapeDtypeStruct((n_batch, FFN_CONV - 1, d_ff), F32))
        scratch = [pltpu.VMEM((tm + 8, tf), F32), pltpu.VMEM((n_chunks, 8, tf), F32)]
    return pl.pallas_call(
        functools.partial(_ffn_kernel, stepwise=stepwise, final_norm=final_norm, n_chunks=n_chunks),
        grid=(n_batch, nt),
        in_specs=in_specs,
        out_specs=out_specs,
        out_shape=out_shape,
        scratch_shapes=scratch,
        compiler_params=_params(2),
        name="outproj_ffn_step" if stepwise else "outproj_ffn",
    )(*args)


def _rope_tables(pos):
    def cos_sin(half):
        freqs = ROPE_BASE ** (-jnp.arange(half, dtype=F32) / half)
        ang = pos[:, None] * freqs[None, :]
        return jnp.cos(ang), jnp.sin(ang)

    def three(cos, sin, reps, width):
        zero = jnp.zeros_like(sin)
        tabs = [jnp.concatenate([cos, cos], 1), jnp.concatenate([-sin, zero], 1),
                jnp.concatenate([zero, sin], 1)]
        tabs = [jnp.tile(t, (1, reps)) for t in tabs]
        tabs = [jnp.pad(t, ((0, 0), (0, width - t.shape[1]))) for t in tabs]
        return jnp.stack(tabs)

    cos_r, sin_r = cos_sin(RET_DK // 2)
    cos_m, sin_m = cos_sin(MLA_ROPE // 2)
    return (three(cos_r, sin_r, RET_HEADS, RET_W), three(cos_m, sin_m, MLA_HEADS, QR_W),
            three(cos_m, sin_m, 1, LANE))


def _retention_tables(chunk):
    lg = jnp.log(1.0 - 2.0 ** (-5.0 - jnp.arange(RET_HEADS, dtype=F32)))
    idx = jnp.arange(chunk, dtype=F32)
    rel = idx[:, None] - idx[None, :]
    decay = jnp.where(rel >= 0, jnp.exp(lg[:, None, None] * jnp.maximum(rel, 0.0)), 0.0)
    per_lane = lambda t: jnp.repeat(t, RET_DK, axis=1)
    gq = per_lane(jnp.exp(lg[None, :] * (idx[:, None] + 1.0)))
    wk = per_lane(jnp.exp(lg[None, :] * (chunk - 1.0 - idx[:, None])))
    gl = jnp.broadcast_to(jnp.repeat(jnp.exp(lg * chunk), RET_DK)[:, None], (RET_W, RET_W))
    head = jnp.arange(RET_W) // RET_DK
    bd = (head[:, None] == head[None, :]).astype(F32)
    gam4 = jnp.broadcast_to(jnp.exp(lg * 1.0)[None, :, None, None], (1, RET_HEADS, RET_DK, RET_DV))
    return dict(chunk=chunk, decay=decay, gq=gq, wk=wk, gl=gl, bd=bd, avg=bd / RET_DV, gam4=gam4)


def _block_diag(w):
    n, c, dd = w.shape
    out = jnp.zeros((n * c, n * dd), w.dtype)
    for i in range(n):
        out = out.at[i * c:(i + 1) * c, i * dd:(i + 1) * dd].set(w[i])
    return out


def _layer_weights(l, P):
    w_in = P["w_in"][l]
    bounds = [0, 256, 512, 768, 1024, 1280, 1408, 1440, 1696, 1952]
    kr_lo, kr_hi = bounds[6], bounds[7]
    w_in_p = jnp.concatenate(
        [w_in[:, :kr_lo], w_in[:, kr_hi:], w_in[:, kr_lo:kr_hi],
         jnp.zeros((w_in.shape[0], _N_IN_PAD - w_in.shape[1]), w_in.dtype)], axis=1)
    w_uq = P["mla_w_uq"][l]
    w_uk = P["mla_w_uk"][l]
    w_uv = P["mla_w_uv"][l]
    uk_pair = jnp.stack([_block_diag(jnp.stack([w_uk[:, 2 * p, :].T, w_uk[:, 2 * p + 1, :].T]))
                         for p in range(MLA_HEADS // 2)])
    uv_pair = jnp.stack([_block_diag(jnp.stack([w_uv[:, 2 * p, :], w_uv[:, 2 * p + 1, :]]))
                         for p in range(MLA_HEADS // 2)])
    row = lambda v: v.reshape(1, -1)
    return dict(
        norm1_w=row(P["norm1_w"][l]), w_in=w_in_p.astype(BF16),
        q_norm_w=row(P["mla_q_norm_w"][l]), kv_norm_w=row(P["mla_kv_norm_w"][l]),
        w_uq_nope=w_uq[:, :, :MLA_NOPE].reshape(MLA_Q_RANK, QN_W).astype(BF16),
        w_uq_rope=w_uq[:, :, MLA_NOPE:].reshape(MLA_Q_RANK, QR_W).astype(BF16),
        w_uk_pair=uk_pair.astype(BF16), w_uv_pair=uv_pair.astype(BF16),
        ret_gn_w=row(P["ret_gn_w"][l]),
        rg_conv_w=P["rg_conv_w"][l], rg_conv_b=row(P["rg_conv_b"][l]),
        rg_wa_bd=_block_diag(P["rg_w_a"][l]).astype(BF16), rg_b_a=row(P["rg_b_a"][l]),
        rg_wx_bd=_block_diag(P["rg_w_x"][l]).astype(BF16), rg_b_x=row(P["rg_b_x"][l]),
        rg_lambda=row(P["rg_lambda"][l]),
        w_out=P["w_out"][l].astype(BF16), norm2_w=row(P["norm2_w"][l]),
        ffn_w_up=P["ffn_w_up"][l].astype(BF16), ffn_conv_w=P["ffn_conv_w"][l],
        ffn_conv_b=row(P["ffn_conv_b"][l]), ffn_w_down=P["ffn_w_down"][l].astype(BF16),
    )


def _tile(n, pref):
    t = min(n, pref)
    assert n % t == 0, (n, pref)
    return t


def kernel(x_prompt, x_sample, cache_mla_latent, cache_mla_krope, page_table, state_ret, state_rg_conv,
           state_rglru, state_ffn_conv, norm1_w, w_in, ret_gn_w, mla_q_norm_w, mla_w_uq, mla_kv_norm_w,
           mla_w_uk, mla_w_uv, rg_conv_w, rg_conv_b, rg_w_a, rg_b_a, rg_w_x, rg_b_x, rg_lambda, w_out,
           norm2_w, ffn_w_up, ffn_conv_w, ffn_conv_b, ffn_w_down, final_norm_w):
    P = dict(norm1_w=norm1_w, w_in=w_in, ret_gn_w=ret_gn_w, mla_q_norm_w=mla_q_norm_w, mla_w_uq=mla_w_uq,
             mla_kv_norm_w=mla_kv_norm_w, mla_w_uk=mla_w_uk, mla_w_uv=mla_w_uv, rg_conv_w=rg_conv_w,
             rg_conv_b=rg_conv_b, rg_w_a=rg_w_a, rg_b_a=rg_b_a, rg_w_x=rg_w_x, rg_b_x=rg_b_x,
             rg_lambda=rg_lambda, w_out=w_out, norm2_w=norm2_w, ffn_w_up=ffn_w_up, ffn_conv_w=ffn_conv_w,
             ffn_conv_b=ffn_conv_b, ffn_w_down=ffn_w_down)
    depth = w_in.shape[0]
    B, S, D = x_prompt.shape
    DB, T, _ = x_sample.shape
    assert T == 1, "the sample path handles one new token per sequence"
    n_pages, page = page_table.shape[1], cache_mla_latent.shape[2]
    past_len = n_pages * page
    final_w = final_norm_w.reshape(1, D)
    weights = [_layer_weights(l, P) for l in range(depth)]

    chunk = RET_CHUNK if (S > RET_CHUNK and S % RET_CHUNK == 0) else S
    rt = _retention_tables(chunk)
    tabs_p = _rope_tables(jnp.arange(S, dtype=F32))
    tm_a, tt, tq, tm_e = _tile(S, 512), _tile(S, 512), _tile(S, 256), _tile(S, 256)
    kb = _tile(S, 512)
    x = x_prompt.reshape(B * S, D)
    p_new = []
    for l in range(depth):
        lw = weights[l]
        ret_in, qh, kcat, p_lat, p_rope, rg_in = _inproj(x, B, tm_a, lw, tabs_p)
        y_ret, sbd = _retention_prompt(ret_in, B, lw, rt)
        y_rg, p_rgc, p_rgh = _rglru_prompt(rg_in, B, tt, lw)
        y_mla = _attention_prompt(qh, kcat, B, tq, kb, lw)
        x, p_ffc = _outproj_ffn(x, y_ret, y_mla, y_rg, lw, B, tm_e,
                                final_w if l == depth - 1 else None)
        s_ret = jnp.stack([sbd[:, h * RET_DK:(h + 1) * RET_DK, h * RET_DV:(h + 1) * RET_DV]
                           for h in range(RET_HEADS)], axis=1)
        p_new.append((p_lat.reshape(B, S, -1), p_rope.reshape(B, S, -1), s_ret, p_rgc,
                      p_rgh.reshape(B, -1), p_ffc))
    y_prompt = x.reshape(B, S, D)

    tabs_s = _rope_tables(jnp.full((DB,), past_len, dtype=F32))
    pages_per_chunk = _tile(n_pages, 64)
    krope_t = jnp.swapaxes(cache_mla_krope, 2, 3)
    x = x_sample.reshape(DB, D)
    s_new = []
    for l in range(depth):
        lw = weights[l]
        ret_in, qh, kcat, s_lat, s_rope, rg_in = _inproj(x, 1, DB, lw, tabs_s)
        y_ret, s_ret = _retention_step(ret_in, state_ret[l], lw, rt)
        y_rg, s_rgh = _rglru_step(rg_in, state_rg_conv[l], state_rglru[l], lw)
        o_lat = _attention_paged(qh, kcat, cache_mla_latent, krope_t, page_table, l,
                                 pages_per_chunk)
        y_mla = _uv_step(o_lat, lw)
        x, gate = _outproj_ffn(x, y_ret, y_mla, y_rg, lw, 1, DB,
                               final_w if l == depth - 1 else None, step_buf=state_ffn_conv[l])
        s_rgc = jnp.concatenate([state_rg_conv[l][:, 1:], rg_in[:, None, 0:RG_WIDTH]], axis=1)
        s_ffc = jnp.concatenate([state_ffn_conv[l][:, 1:], gate[:, None, :]], axis=1)
        s_new.append((s_lat.reshape(DB, T, -1), s_rope.reshape(DB, T, -1), s_ret, s_rgc, s_rgh, s_ffc))
    y_sample = x.reshape(DB, T, D)

    stack = lambda items, i: jnp.stack([it[i] for it in items])
    return (y_prompt, y_sample, *[stack(p_new, i) for i in range(6)], *[stack(s_new, i) for i in range(6)])
```

```python
import functools

import jax
import jax.numpy as jnp
from jax import lax
from jax.experimental import pallas as pl
from jax.experimental.pallas import tpu as pltpu

RET_HEADS = 4
RET_DK = 64
RET_DV = 64
RET_CHUNK = 128
MLA_HEADS = 8
MLA_NOPE = 64
MLA_ROPE = 32
MLA_V = 64
MLA_Q_RANK = 256
MLA_KV_RANK = 128
MLA_SCALE = (MLA_NOPE + MLA_ROPE) ** -0.5
LOG2_E = 1.4426950408889634
Q_SCALE = MLA_SCALE * LOG2_E
RG_WIDTH = 256
RG_BLOCKS = 4
RG_CONV = 4
RG_C = 8.0
FFN_CONV = 3
ROPE_BASE = 10000.0
EPS = 1e-6
NEG_INF = -1e30

RET_W = RET_HEADS * RET_DK
QR_W = MLA_HEADS * MLA_ROPE
QN_W = MLA_HEADS * MLA_NOPE
KCAT_W = 256
LANE = 128
VMEM_LIMIT = 56 * 1024 * 1024
PAGED_SLOTS = 3

BF16 = jnp.bfloat16
F32 = jnp.float32

_OFF_RQ, _OFF_RK, _OFF_RV, _OFF_RG = 0, 256, 512, 768
_OFF_CQ, _OFF_CKV, _OFF_GX, _OFF_GG, _OFF_KR = 1024, 1280, 1408, 1664, 1920
_N_IN_PAD = 2048


def _params(n_axes):
    return pltpu.CompilerParams(dimension_semantics=("arbitrary",) * n_axes,
                                vmem_limit_bytes=VMEM_LIMIT)


def _const_spec(shape):
    n = len(shape)
    return pl.BlockSpec(shape, lambda *_: (0,) * n, pipeline_mode=pl.Buffered(1))


def _dot(a, b):
    return jnp.dot(a, b, preferred_element_type=F32)


def _dot_nt(a, b):
    return lax.dot_general(a, b, (((1,), (1,)), ((), ())), preferred_element_type=F32)


def _dot_tn(a, b):
    return lax.dot_general(a, b, (((0,), (0,)), ((), ())), preferred_element_type=F32)


def _rms(x, w):
    return x * lax.rsqrt(jnp.mean(x * x, axis=-1, keepdims=True) + EPS) * w


def _rope3(x, tab_ref, half):
    w = x.shape[-1]
    return (x * tab_ref[0] + pltpu.roll(x, w - half, 1) * tab_ref[1]
            + pltpu.roll(x, half, 1) * tab_ref[2])


def _inproj_kernel(x_ref, n1_ref, win_ref, tabr_ref, tabq_ref, tabk_ref, qnw_ref, wuqn_ref,
                   wuqr_ref, wukp_ref, kvnw_ref,
                   ret_ref, q_ref, kcat_ref, plat_ref, prope_ref, rg_ref):
    h = _rms(x_ref[...], n1_ref[...]).astype(BF16)
    z = _dot(h, win_ref[...])

    ret_ref[:, _OFF_RQ:_OFF_RQ + RET_W] = _rope3(z[:, _OFF_RQ:_OFF_RQ + RET_W], tabr_ref, RET_DK // 2)
    ret_ref[:, _OFF_RK:_OFF_RK + RET_W] = (
        _rope3(z[:, _OFF_RK:_OFF_RK + RET_W], tabr_ref, RET_DK // 2) * (RET_DK ** -0.5))
    ret_ref[:, _OFF_RV:_OFF_CQ] = z[:, _OFF_RV:_OFF_CQ]
    rg_ref[...] = z[:, _OFF_GX:_OFF_KR]

    ckvn = _rms(z[:, _OFF_CKV:_OFF_CKV + MLA_KV_RANK], kvnw_ref[...])
    plat_ref[...] = ckvn
    krr = _rope3(z[:, _OFF_KR:_OFF_KR + LANE], tabk_ref, MLA_ROPE // 2)
    prope_ref[...] = krr[:, :MLA_ROPE]
    kcat_ref[:, 0:MLA_KV_RANK] = ckvn.astype(BF16)
    kcat_ref[:, MLA_KV_RANK:KCAT_W] = krr.astype(BF16)

    cqn = _rms(z[:, _OFF_CQ:_OFF_CQ + MLA_Q_RANK], qnw_ref[...]).astype(BF16)
    qn = _dot(cqn, wuqn_ref[...]).astype(BF16)
    qr = _rope3(_dot(cqn, wuqr_ref[...]), tabq_ref, MLA_ROPE // 2) * Q_SCALE
    lane = lax.broadcasted_iota(jnp.int32, (1, LANE), 1)
    heads_per_tile = LANE // MLA_ROPE
    for p in range(MLA_HEADS // 2):
        ql = _dot(qn[:, p * LANE:(p + 1) * LANE], wukp_ref[p]) * Q_SCALE
        for e in range(2):
            hd = 2 * p + e
            q_ref[hd, :, 0:MLA_KV_RANK] = ql[:, e * LANE:(e + 1) * LANE].astype(BF16)
            tile = qr[:, (hd // heads_per_tile) * LANE:(hd // heads_per_tile + 1) * LANE]
            sh = MLA_ROPE * (hd % heads_per_tile)
            if sh:
                tile = pltpu.roll(tile, LANE - sh, 1)
            q_ref[hd, :, MLA_KV_RANK:KCAT_W] = jnp.where(lane < MLA_ROPE, tile, 0.0).astype(BF16)


def _inproj(x2d, n_batch, tm, lw, tabs):
    rows, d = x2d.shape
    ns = rows // (n_batch * tm)
    tabr, tabq, tabk = tabs
    row_map = lambda s, b: (b * ns + s, 0)
    tab_map = lambda s, b: (0, s, 0)
    out_shape = (
        jax.ShapeDtypeStruct((rows, 4 * RET_W), F32),
        jax.ShapeDtypeStruct((MLA_HEADS, rows, KCAT_W), BF16),
        jax.ShapeDtypeStruct((rows, KCAT_W), BF16),
        jax.ShapeDtypeStruct((rows, MLA_KV_RANK), F32),
        jax.ShapeDtypeStruct((rows, MLA_ROPE), F32),
        jax.ShapeDtypeStruct((rows, 2 * RG_WIDTH), F32),
    )
    return pl.pallas_call(
        _inproj_kernel,
        grid=(ns, n_batch),
        in_specs=[
            pl.BlockSpec((tm, d), row_map),
            _const_spec((1, d)),
            _const_spec((d, _N_IN_PAD)),
            pl.BlockSpec((3, tm, RET_W), tab_map),
            pl.BlockSpec((3, tm, QR_W), tab_map),
            pl.BlockSpec((3, tm, LANE), tab_map),
            _const_spec((1, MLA_Q_RANK)),
            _const_spec((MLA_Q_RANK, QN_W)),
            _const_spec((MLA_Q_RANK, QR_W)),
            _const_spec((MLA_HEADS // 2, LANE, 2 * MLA_KV_RANK)),
            _const_spec((1, MLA_KV_RANK)),
        ],
        out_specs=[
            pl.BlockSpec((tm, 4 * RET_W), row_map),
            pl.BlockSpec((MLA_HEADS, tm, KCAT_W), lambda s, b: (0, b * ns + s, 0)),
            pl.BlockSpec((tm, KCAT_W), row_map),
            pl.BlockSpec((tm, MLA_KV_RANK), row_map),
            pl.BlockSpec((tm, MLA_ROPE), row_map),
            pl.BlockSpec((tm, 2 * RG_WIDTH), row_map),
        ],
        out_shape=out_shape,
        compiler_params=_params(2),
        name="inproj",
    )(x2d, lw["norm1_w"], lw["w_in"], tabr, tabq, tabk, lw["q_norm_w"], lw["w_uq_nope"],
      lw["w_uq_rope"], lw["w_uk_pair"], lw["kv_norm_w"])


def _group_norm_gate(o, g, gnw, avg):
    mu = _dot(o, avg)
    d = o - mu
    var = _dot(d * d, avg)
    return d * lax.rsqrt(var + EPS) * gnw * (g * jax.nn.sigmoid(g))


def _ret_kernel(ret_ref, dec_ref, gq_ref, wk_ref, gl_ref, bd_ref, gnw_ref, avg_ref,
                y_ref, sbd_ref, s_scr):
    @pl.when(pl.program_id(1) == 0)
    def _():
        s_scr[...] = jnp.zeros_like(s_scr)

    L = dec_ref.shape[1]
    head_of_lane = lax.broadcasted_iota(jnp.int32, (1, RET_W), 1) // RET_DK
    s_old = s_scr[...]
    for cc in range(ret_ref.shape[0] // L):
        rows = pl.ds(cc * L, L)
        q = ret_ref[rows, _OFF_RQ:_OFF_RQ + RET_W]
        k = ret_ref[rows, _OFF_RK:_OFF_RK + RET_W]
        vb = ret_ref[rows, _OFF_RV:_OFF_RV + RET_W].astype(BF16)
        g = ret_ref[rows, _OFF_RG:_OFF_RG + RET_W]
        kb = k.astype(BF16)
        o = _dot(q.astype(BF16), s_old.astype(BF16)) * gq_ref[...]
        for h in range(RET_HEADS):
            mh = head_of_lane == h
            s = _dot_nt(jnp.where(mh, q, 0.0).astype(BF16), kb) * dec_ref[h]
            o = o + jnp.where(mh, _dot(s.astype(BF16), vb), 0.0)
        upd = _dot_tn((k * wk_ref[...]).astype(BF16), vb)
        s_old = gl_ref[...] * s_old + bd_ref[...] * upd
        y_ref[rows, :] = _group_norm_gate(o, g, gnw_ref[...], avg_ref[...])
    s_scr[...] = s_old
    sbd_ref[...] = s_old


def _retention_prompt(ret_in, n_batch, lw, rt, chunks_per_step):
    rows = ret_in.shape[0]
    L = rt["chunk"]
    tb = L * chunks_per_step
    nc = rows // (n_batch * tb)
    return pl.pallas_call(
        _ret_kernel,
        grid=(n_batch, nc),
        in_specs=[
            pl.BlockSpec((tb, 4 * RET_W), lambda b, c: (b * nc + c, 0)),
            _const_spec((RET_HEADS, L, L)),
            _const_spec((L, RET_W)),
            _const_spec((L, RET_W)),
            _const_spec((RET_W, RET_W)),
            _const_spec((RET_W, RET_W)),
            _const_spec((1, RET_W)),
            _const_spec((RET_W, RET_W)),
        ],
        out_specs=[
            pl.BlockSpec((tb, RET_W), lambda b, c: (b * nc + c, 0)),
            pl.BlockSpec((None, RET_W, RET_W), lambda b, c: (b, 0, 0)),
        ],
        out_shape=(jax.ShapeDtypeStruct((rows, RET_W), F32),
                   jax.ShapeDtypeStruct((n_batch, RET_W, RET_W), F32)),
        scratch_shapes=[pltpu.VMEM((RET_W, RET_W), F32)],
        compiler_params=_params(2),
        name="retention",
    )(ret_in, rt["decay"], rt["gq"], rt["wk"], rt["gl"], rt["bd"], lw["ret_gn_w"], rt["avg"])


def _ret_step_kernel(q_ref, k_ref, v_ref, g_ref, s0_ref, gam_ref, gnw_ref, y_ref, s_ref):
    s_new = gam_ref[...] * s0_ref[...] + k_ref[...] * v_ref[...]
    s_ref[...] = s_new
    o = jnp.sum(q_ref[...] * s_new, axis=2, keepdims=True)
    mu = jnp.mean(o, axis=3, keepdims=True)
    d = o - mu
    var = jnp.mean(d * d, axis=3, keepdims=True)
    g = g_ref[...]
    y_ref[...] = d * lax.rsqrt(var + EPS) * gnw_ref[...] * (g * jax.nn.sigmoid(g))


def _retention_step(ret_in, s0, lw, rt):
    n = ret_in.shape[0]
    bs = _tile(n, 8)
    q4 =ret_in[:, _OFF_RQ:_OFF_RQ + RET_W].reshape(n, RET_HEADS, RET_DK, 1)
    k4 = ret_in[:, _OFF_RK:_OFF_RK + RET_W].reshape(n, RET_HEADS, RET_DK, 1)
    v4 = ret_in[:, _OFF_RV:_OFF_RV + RET_W].reshape(n, RET_HEADS, 1, RET_DV)
    g4 = ret_in[:, _OFF_RG:_OFF_RG + RET_W].reshape(n, RET_HEADS, 1, RET_DV)
    col = pl.BlockSpec((bs, RET_HEADS, RET_DK, 1), lambda i: (i, 0, 0, 0))
    row = pl.BlockSpec((bs, RET_HEADS, 1, RET_DV), lambda i: (i, 0, 0, 0))
    mat = pl.BlockSpec((bs, RET_HEADS, RET_DK, RET_DV), lambda i: (i, 0, 0, 0))
    y4, s_new = pl.pallas_call(
        _ret_step_kernel,
        grid=(n // bs,),
        in_specs=[col, col, row, row, mat,
                  _const_spec((1, RET_HEADS, RET_DK, RET_DV)),
                  _const_spec((1, RET_HEADS, 1, RET_DV))],
        out_specs=[row, mat],
        out_shape=(jax.ShapeDtypeStruct((n, RET_HEADS, 1, RET_DV), F32),
                   jax.ShapeDtypeStruct((n, RET_HEADS, RET_DK, RET_DV), F32)),
        compiler_params=_params(1),
        name="retention_step",
    )(q4, k4, v4, g4, s0, rt["gam4"], lw["ret_gn_w"].reshape(1, RET_HEADS, 1, RET_DV))
    return y4.reshape(n, RET_W), s_new


def _rg_gates(xc, wa_ref, ba_ref, wx_ref, bx_ref, lam_ref):
    xb = xc.astype(BF16)
    r = jax.nn.sigmoid(_dot(xb, wa_ref[...]) + ba_ref[...])
    i = jax.nn.sigmoid(_dot(xb, wx_ref[...]) + bx_ref[...])
    y = -lam_ref[...]
    softplus = jnp.maximum(y, 0.0) + jnp.log1p(jnp.exp(-jnp.abs(y)))
    log_a = -RG_C * r * softplus
    a = jnp.exp(log_a)
    u = jnp.sqrt(1.0 - jnp.exp(2.0 * log_a)) * i * xc
    return a, u


def _rg_kernel(rg_ref, cw_ref, cb_ref, wa_ref, ba_ref, wx_ref, bx_ref, lam_ref,
               y_ref, buf_ref, hlast_ref, xs_scr, h_scr):
    tt = rg_ref.shape[0]
    halo = 8

    @pl.when(pl.program_id(1) == 0)
    def _():
        xs_scr[0:halo, :] = jnp.zeros((halo, RG_WIDTH), F32)
        h_scr[...] = jnp.zeros_like(h_scr)

    gx = rg_ref[:, 0:RG_WIDTH]
    gg = rg_ref[:, RG_WIDTH:2 * RG_WIDTH]
    xs_scr[halo:halo + tt, :] = gx
    xc = cb_ref[...] + gx * cw_ref[RG_CONV - 1:RG_CONV, :]
    for j in range(1, RG_CONV):
        xc = xc + xs_scr[halo - j:halo - j + tt, :] * cw_ref[RG_CONV - 1 - j:RG_CONV - j, :]
    xs_scr[0:halo, :] = xs_scr[tt:tt + halo, :]
    buf_ref[...] = gx[tt - (RG_CONV - 1):tt, :]

    a, u = _rg_gates(xc, wa_ref, ba_ref, wx_ref, bx_ref, lam_ref)
    row = lax.broadcasted_iota(jnp.int32, (tt, 1), 0)
    step = 1
    while step < tt:
        keep = row >= step
        a_sh = jnp.where(keep, pltpu.roll(a, step, 0), 1.0)
        u_sh = jnp.where(keep, pltpu.roll(u, step, 0), 0.0)
        u = a * u_sh + u
        a = a * a_sh
        step *= 2
    h = u + a * h_scr[...]
    h_scr[...] = h[tt - 1:tt, :]
    hlast_ref[...] = h[tt - 1:tt, :]
    y_ref[...] = h * jax.nn.gelu(gg, approximate=True)


def _rglru_prompt(rg_in, n_batch, tt, lw):
    rows = rg_in.shape[0]
    nt = rows // (n_batch * tt)
    w = RG_WIDTH
    return pl.pallas_call(
        _rg_kernel,
        grid=(n_batch, nt),
        in_specs=[
            pl.BlockSpec((tt, 2 * w), lambda b, t: (b * nt + t, 0)),
            _const_spec((RG_CONV, w)), _const_spec((1, w)),
            _const_spec((w, w)), _const_spec((1, w)),
            _const_spec((w, w)), _const_spec((1, w)), _const_spec((1, w)),
        ],
        out_specs=[
            pl.BlockSpec((tt, w), lambda b, t: (b * nt + t, 0)),
            pl.BlockSpec((None, RG_CONV - 1, w), lambda b, t: (b, 0, 0)),
            pl.BlockSpec((None, 1, w), lambda b, t: (b, 0, 0)),
        ],
        out_shape=(jax.ShapeDtypeStruct((rows, w), F32),
                   jax.ShapeDtypeStruct((n_batch, RG_CONV - 1, w), F32),
                   jax.ShapeDtypeStruct((n_batch, 1, w), F32)),
        scratch_shapes=[pltpu.VMEM((tt + 8, w), F32), pltpu.VMEM((1, w), F32)],
        compiler_params=_params(2),
        name="rglru",
    )(rg_in, lw["rg_conv_w"], lw["rg_conv_b"], lw["rg_wa_bd"], lw["rg_b_a"], lw["rg_wx_bd"],
      lw["rg_b_x"], lw["rg_lambda"])


def _rg_step_kernel(rg_ref, b0_ref, b1_ref, b2_ref, h0_ref, cw_ref, cb_ref, wa_ref, ba_ref,
                    wx_ref, bx_ref, lam_ref, y_ref, h_ref):
    gx = rg_ref[:, 0:RG_WIDTH]
    gg = rg_ref[:, RG_WIDTH:2 * RG_WIDTH]
    xc = (cb_ref[...] + b0_ref[...] * cw_ref[0:1, :] + b1_ref[...] * cw_ref[1:2, :]
          + b2_ref[...] * cw_ref[2:3, :] + gx * cw_ref[3:4, :])
    a, u = _rg_gates(xc, wa_ref, ba_ref, wx_ref, bx_ref, lam_ref)
    h = u + a * h0_ref[...]
    h_ref[...] = h
    y_ref[...] = h * jax.nn.gelu(gg, approximate=True)


def _rglru_step(rg_in, buf0, h0, lw):
    n = rg_in.shape[0]
    w = RG_WIDTH
    full = lambda shape: pl.BlockSpec(shape, lambda i: (0,) * len(shape))
    return pl.pallas_call(
        _rg_step_kernel,
        grid=(1,),
        in_specs=[full((n, 2 * w)), full((n, w)), full((n, w)), full((n, w)), full((n, w)),
                  full((RG_CONV, w)), full((1, w)), full((w, w)), full((1, w)), full((w, w)),
                  full((1, w)), full((1, w))],
        out_specs=[full((n, w)), full((n, w))],
        out_shape=(jax.ShapeDtypeStruct((n, w), F32), jax.ShapeDtypeStruct((n, w), F32)),
        compiler_params=_params(1),
        name="rglru_step",
    )(rg_in, buf0[:, 0], buf0[:, 1], buf0[:, 2], h0, lw["rg_conv_w"], lw["rg_conv_b"],
      lw["rg_wa_bd"], lw["rg_b_a"], lw["rg_wx_bd"], lw["rg_b_x"], lw["rg_lambda"])


def _uv_project(o, tq, wuv_ref, out_ref):
    for p in range(MLA_HEADS // 2):
        pair = jnp.concatenate([o[(2 * p) * tq:(2 * p + 1) * tq], o[(2 * p + 1) * tq:(2 * p + 2) * tq]],
                               axis=1).astype(BF16)
        out_ref[:, p * LANE:(p + 1) * LANE] = _dot(pair, wuv_ref[p])


def _attn_kernel(q_ref, k_ref, wuv_ref, o_ref, m_scr, l_scr, acc_scr, *, kb):
    tq = q_ref.shape[1]
    m_rows = MLA_HEADS * tq
    i = pl.program_id(1)
    q = q_ref[...].reshape(m_rows, KCAT_W)
    m_scr[...] = jnp.full_like(m_scr, NEG_INF)
    l_scr[...] = jnp.zeros_like(l_scr)
    acc_scr[...] = jnp.zeros_like(acc_scr)

    def block(start, width, masked):
        kblk = k_ref[pl.ds(pl.multiple_of(start, tq), width), :]
        s = _dot_nt(q, kblk)
        if masked:
            qpos = lax.broadcasted_iota(jnp.int32, (MLA_HEADS, tq, width), 1).reshape(m_rows, width)
            kpos = lax.broadcasted_iota(jnp.int32, (m_rows, width), 1)
            s = jnp.where(kpos <= qpos, s, NEG_INF)
        m_prev = m_scr[...]
        m_next = jnp.maximum(m_prev, jnp.max(s, axis=1, keepdims=True))
        p = jnp.exp2(s - jnp.concatenate([m_next] * (width // LANE), axis=1))
        alpha = jnp.exp2(m_prev - m_next)
        l_scr[...] = alpha * l_scr[...] + jnp.sum(p, axis=1, keepdims=True)
        acc_scr[...] = alpha * acc_scr[...] + _dot(p.astype(BF16), kblk[:, 0:MLA_KV_RANK])
        m_scr[...] = m_next

    n_wide = (i * tq) // kb
    n_narrow = (i * tq - n_wide * kb) // tq

    def wide(j, carry):
        block(j * kb, kb, False)
        return carry

    def narrow(j, carry):
        block(n_wide * kb + j * tq, tq, False)
        return carry

    lax.fori_loop(0, n_wide, wide, 0)
    lax.fori_loop(0, n_narrow, narrow, 0)
    block(i * tq, tq, True)
    o = acc_scr[...] / l_scr[...]
    _uv_project(o, tq, wuv_ref, o_ref)


def _attention_prompt(qh, kcat, n_batch, tq, kb, lw):
    rows = kcat.shape[0]
    s_len = rows // n_batch
    nq = s_len // tq
    m_rows = MLA_HEADS * tq
    return pl.pallas_call(
        functools.partial(_attn_kernel, kb=kb),
        grid=(n_batch, nq),
        in_specs=[
            pl.BlockSpec((MLA_HEADS, tq, KCAT_W), lambda b, i: (0, b * nq + i, 0)),
            pl.BlockSpec((s_len, KCAT_W), lambda b, i: (b, 0)),
            _const_spec((MLA_HEADS // 2, 2 * MLA_KV_RANK, LANE)),
        ],
        out_specs=pl.BlockSpec((tq, MLA_HEADS * MLA_V), lambda b, i: (b * nq + i, 0)),
        out_shape=jax.ShapeDtypeStruct((rows, MLA_HEADS * MLA_V), F32),
        scratch_shapes=[pltpu.VMEM((m_rows, LANE), F32), pltpu.VMEM((m_rows, LANE), F32),
                        pltpu.VMEM((m_rows, MLA_KV_RANK), F32)],
        compiler_params=_params(2),
        name="attention",
    )(qh, kcat, lw["w_uv_pair"])


def _paged_kernel(pt_ref, q_ref, kself_ref, lat_hbm, krt_hbm, o_ref,
                  lat_buf, krt_buf, sem, m_scr, l_scr, acc_scr, *, layer, pages_per_chunk,
                  chunks_per_seq, sub_keys):
    g = pl.program_id(0)
    page = lat_hbm.shape[2]
    chunk_keys = pages_per_chunk * page
    n_slots = lat_buf.shape[0]
    ahead = n_slots - 1

    last = pl.num_programs(0) - 1
    slot = g % n_slots
    fill = (g + ahead) % n_slots
    n_sub = chunk_keys // sub_keys
    pages_per_sub = pages_per_chunk // n_sub

    def page_copies(pid, dst_slot, j):
        cols = pl.ds(j * page, page)
        return (pltpu.make_async_copy(lat_hbm.at[layer, pid], lat_buf.at[dst_slot, cols, :], sem.at[dst_slot, 0]),
                pltpu.make_async_copy(krt_hbm.at[layer, pid], krt_buf.at[dst_slot, :, cols], sem.at[dst_slot, 1]))

    def start_pages(step, dst_slot, pages):
        for j in pages:
            for cp in page_copies(pt_ref[step * pages_per_chunk + j], dst_slot, j):
                cp.start()

    def wait_chunk(dst_slot):
        for j in range(pages_per_chunk):
            for cp in page_copies(0, dst_slot, j):
                cp.wait()

    @pl.when(g == 0)
    def _():
        for a in range(ahead):
            start_pages(jnp.minimum(a, last), a, range(pages_per_chunk))

    wait_chunk(slot)
    c = g % chunks_per_seq

    @pl.when(c == 0)
    def _():
        m_scr[...] = jnp.full_like(m_scr, NEG_INF)
        l_scr[...] = jnp.zeros_like(l_scr)
        acc_scr[...] = jnp.zeros_like(acc_scr)

    q = q_ref[...]
    q_lat = q[:, 0:MLA_KV_RANK]
    q_rope = q[:, MLA_KV_RANK:MLA_KV_RANK + MLA_ROPE]
    nxt = jnp.minimum(g + ahead, last)
    lats, scores = [], []
    for sb in range(n_sub):
        start_pages(nxt, fill, range(sb * pages_per_sub, (sb + 1) * pages_per_sub))
        keys = pl.ds(sb * sub_keys, sub_keys)
        lat = lat_buf[slot, keys, :].astype(BF16)
        krt = krt_buf[slot, :, keys].astype(BF16)
        lats.append(lat)
        scores.append(_dot_nt(q_lat, lat) + _dot(q_rope, krt))
    m_prev = m_scr[...]
    m_cur = functools.reduce(jnp.maximum, [jnp.max(s, axis=1, keepdims=True) for s in scores])
    m_next = jnp.maximum(m_prev, m_cur)
    alpha = jnp.exp2(m_prev - m_next)
    l_new = alpha * l_scr[...]
    acc = alpha * acc_scr[...]
    for s, lat in zip(scores, lats):
        p = jnp.exp2(s - m_next)
        l_new = l_new + jnp.sum(p, axis=1, keepdims=True)
        acc = acc + _dot(p.astype(BF16), lat)
    m_scr[...] = m_next
    l_scr[...] = l_new
    acc_scr[...] = acc

    @pl.when(g == last)
    def _():
        for a in range(1, n_slots):
            wait_chunk((g + a) % n_slots)

    @pl.when(c == chunks_per_seq - 1)
    def _():
        kself = kself_ref[...].astype(F32)
        s_self = jnp.sum(q.astype(F32) * kself, axis=1, keepdims=True)
        m_prev = m_scr[...]
        m_fin = jnp.maximum(m_prev, s_self)
        alpha = jnp.exp2(m_prev - m_fin)
        p_self = jnp.exp2(s_self - m_fin)
        l_fin = alpha * l_scr[...] + p_self
        acc_fin = alpha * acc_scr[...] + p_self * kself[:, 0:MLA_KV_RANK]
        o_ref[...] = acc_fin / l_fin


def _attention_paged(qh, kcat, pool_lat, pool_rope_t, page_table, layer, pages_per_chunk):
    n = kcat.shape[0]
    n_pages = page_table.shape[1]
    page, rank = pool_lat.shape[2], pool_lat.shape[3]
    rope = pool_rope_t.shape[2]
    ppc = pages_per_chunk
    cps = n_pages // ppc
    chunk_keys = ppc * page
    q3 = jnp.transpose(qh, (1, 0, 2))
    kself = kcat.reshape(n, 1, KCAT_W)
    pt_flat = page_table.reshape(-1)
    seq_map = lambda g, pt: (g // cps, 0, 0)
    grid_spec = pltpu.PrefetchScalarGridSpec(
        num_scalar_prefetch=1,
        grid=(n * cps,),
        in_specs=[pl.BlockSpec((None, MLA_HEADS, KCAT_W), seq_map),
                  pl.BlockSpec((None, 1, KCAT_W), seq_map),
                  pl.BlockSpec(memory_space=pl.ANY),
                  pl.BlockSpec(memory_space=pl.ANY)],
        out_specs=pl.BlockSpec((None, MLA_HEADS, MLA_KV_RANK), seq_map),
        scratch_shapes=[pltpu.VMEM((PAGED_SLOTS, chunk_keys, rank), F32),
                        pltpu.VMEM((PAGED_SLOTS, rope, chunk_keys), F32),
                        pltpu.SemaphoreType.DMA((PAGED_SLOTS, 2)),
                        pltpu.VMEM((MLA_HEADS, 1), F32), pltpu.VMEM((MLA_HEADS, 1), F32),
                        pltpu.VMEM((MLA_HEADS, MLA_KV_RANK), F32)],
    )
    return pl.pallas_call(
        functools.partial(_paged_kernel, layer=layer, pages_per_chunk=ppc, chunks_per_seq=cps,
                          sub_keys=min(chunk_keys, 2048)),
        grid_spec=grid_spec,
        out_shape=jax.ShapeDtypeStruct((n, MLA_HEADS, MLA_KV_RANK), F32),
        compiler_params=_params(1),
        name="attention_paged",
    )(pt_flat, q3, kself, pool_lat, pool_rope_t)


def _uv_kernel(o_ref, wuv_ref, y_ref):
    for p in range(MLA_HEADS // 2):
        pair = o_ref[:, 2 * p * MLA_KV_RANK:(2 * p + 2) * MLA_KV_RANK].astype(BF16)
        y_ref[:, p * LANE:(p + 1) * LANE] = _dot(pair, wuv_ref[p])


def _uv_step(o_lat, lw):
    n = o_lat.shape[0]
    o2 = o_lat.reshape(n, MLA_HEADS * MLA_KV_RANK)
    full = lambda shape: pl.BlockSpec(shape, lambda i: (0,) * len(shape))
    return pl.pallas_call(
        _uv_kernel,
        grid=(1,),
        in_specs=[full(o2.shape), full((MLA_HEADS // 2, 2 * MLA_KV_RANK, LANE))],
        out_specs=full((n, MLA_HEADS * MLA_V)),
        out_shape=jax.ShapeDtypeStruct((n, MLA_HEADS * MLA_V), F32),
        compiler_params=_params(1),
        name="uv_step",
    )(o2, lw["w_uv_pair"])


def _ff_bounds(d_ff, max_tiles=4, tile=256):
    if d_ff % tile:
        return (0, d_ff)
    tiles = d_ff // tile
    n = -(-tiles // max_tiles)
    sizes = [tiles // n + (1 if c >= n - tiles % n else 0) for c in range(n)]
    bounds = [0]
    for s in sizes:
        bounds.append(bounds[-1] + s * tile)
    return tuple(bounds)


def _ffn_kernel(*refs, stepwise, final_norm, bounds):
    x_ref, yret_ref, ymla_ref, yrg_ref, wout_ref, n2_ref, wup_ref, cw_ref, cb_ref, wdn_ref = refs[:10]
    pos = 10
    fn_ref = None
    if final_norm:
        fn_ref = refs[pos]
        pos += 1
    if stepwise:
        bm2_ref, bm1_ref = refs[pos:pos + 2]
        out_ref, gate_ref = refs[pos + 2:pos + 4]
    else:
        out_ref, tail_ref, g_scr, carry_scr = refs[pos:pos + 4]
    tm = x_ref.shape[0]
    d_ff = wdn_ref.shape[0]
    halo = 8

    if not stepwise:
        @pl.when(pl.program_id(1) == 0)
        def _():
            carry_scr[...] = jnp.zeros_like(carry_scr)

    mix = jnp.concatenate([yret_ref[...], ymla_ref[...], yrg_ref[...]], axis=1).astype(BF16)
    x1 = x_ref[...] + _dot(mix, wout_ref[...])
    h2 = _rms(x1, n2_ref[...]).astype(BF16)
    acc = jnp.zeros_like(x1)
    for c in range(len(bounds) - 1):
        lo, hi = bounds[c], bounds[c + 1]
        tf = hi - lo
        gate = _dot(h2, wup_ref[:, lo:hi])
        val = _dot(h2, wup_ref[:, d_ff + lo:d_ff + hi])
        if stepwise:
            gm2 = bm2_ref[:, lo:hi]
            gm1 = bm1_ref[:, lo:hi]
            gate_ref[:, lo:hi] = gate
        else:
            g_scr[0:halo, 0:tf] = carry_scr[c, :, 0:tf]
            g_scr[halo:halo + tm, 0:tf] = gate
            gm1 = g_scr[halo - 1:halo - 1 + tm, 0:tf]
            gm2 = g_scr[halo - 2:halo - 2 + tm, 0:tf]
            carry_scr[c, :, 0:tf] = g_scr[tm:tm + halo, 0:tf]
            tail_ref[:, lo:hi] = gate[tm - (FFN_CONV - 1):tm, :]
        gc = (cb_ref[:, lo:hi] + gm2 * cw_ref[0:1, lo:hi] + gm1 * cw_ref[1:2, lo:hi]
              + gate * cw_ref[2:3, lo:hi])
        act = (gc * jax.nn.sigmoid(gc) * val).astype(BF16)
        acc = acc + _dot(act, wdn_ref[lo:hi, :])
    x2 = x1 + acc
    out_ref[...] = _rms(x2, fn_ref[...]) if final_norm else x2


def _outproj_ffn(x2d, y_ret, y_mla, y_rg, lw, n_batch, tm, final_w, step_buf=None):
    rows, d = x2d.shape
    d_ff = lw["ffn_w_down"].shape[0]
    bounds = _ff_bounds(d_ff)
    n_chunks = len(bounds) - 1
    tf = max(bounds[c + 1] - bounds[c] for c in range(n_chunks))
    stepwise = step_buf is not None
    final_norm = final_w is not None
    nt = rows // (n_batch * tm)
    row_map = lambda b, t: (b * nt + t, 0)
    in_specs = [
        pl.BlockSpec((tm, d), row_map),
        pl.BlockSpec((tm, RET_W), row_map),
        pl.BlockSpec((tm, MLA_HEADS * MLA_V), row_map),
        pl.BlockSpec((tm, RG_WIDTH), row_map),
        _const_spec((d, d)), _const_spec((1, d)), _const_spec((d, 2 * d_ff)),
        _const_spec((FFN_CONV, d_ff)), _const_spec((1, d_ff)), _const_spec((d_ff, d)),
    ]
    args = [x2d, y_ret, y_mla, y_rg, lw["w_out"], lw["norm2_w"], lw["ffn_w_up"], lw["ffn_conv_w"],
            lw["ffn_conv_b"], lw["ffn_w_down"]]
    if final_norm:
        in_specs.append(_const_spec((1, d)))
        args.append(final_w)
    if stepwise:
        in_specs += [pl.BlockSpec((tm, d_ff), row_map)] * 2
        args += [step_buf[:, 0], step_buf[:, 1]]
        out_specs = [pl.BlockSpec((tm, d), row_map), pl.BlockSpec((tm, d_ff), row_map)]
        out_shape = (jax.ShapeDtypeStruct((rows, d), F32), jax.ShapeDtypeStruct((rows, d_ff), F32))
        scratch = []
    else:
        out_specs = [pl.BlockSpec((tm, d), row_map),
                     pl.BlockSpec((None, FFN_CONV - 1, d_ff), lambda b, t: (b, 0, 0))]
        out_shape = (jax.ShapeDtypeStruct((rows, d), F32),
                     jax.ShapeDtypeStruct((n_batch, FFN_CONV - 1, d_ff), F32))
        scratch = [pltpu.VMEM((tm + 8, tf), F32), pltpu.VMEM((n_chunks, 8, tf), F32)]
    return pl.pallas_call(
        functools.partial(_ffn_kernel, stepwise=stepwise, final_norm=final_norm, bounds=bounds),
        grid=(n_batch, nt),
        in_specs=in_specs,
        out_specs=out_specs,
        out_shape=out_shape,
        scratch_shapes=scratch,
        compiler_params=_params(2),
        name="outproj_ffn_step" if stepwise else "outproj_ffn",
    )(*args)


def _rope_tables(pos):
    def cos_sin(half):
        freqs = ROPE_BASE ** (-jnp.arange(half, dtype=F32) / half)
        ang = pos[:, None] * freqs[None, :]
        return jnp.cos(ang), jnp.sin(ang)

    def three(cos, sin, reps, width):
        zero = jnp.zeros_like(sin)
        tabs = [jnp.concatenate([cos, cos], 1), jnp.concatenate([-sin, zero], 1),
                jnp.concatenate([zero, sin], 1)]
        tabs = [jnp.tile(t, (1, reps)) for t in tabs]
        tabs = [jnp.pad(t, ((0, 0), (0, width - t.shape[1]))) for t in tabs]
        return jnp.stack(tabs)

    cos_r, sin_r = cos_sin(RET_DK // 2)
    cos_m, sin_m = cos_sin(MLA_ROPE // 2)
    return (three(cos_r, sin_r, RET_HEADS, RET_W), three(cos_m, sin_m, MLA_HEADS, QR_W),
            three(cos_m, sin_m, 1, LANE))


def _retention_tables(chunk):
    lg = jnp.log(1.0 - 2.0 ** (-5.0 - jnp.arange(RET_HEADS, dtype=F32)))
    idx = jnp.arange(chunk, dtype=F32)
    rel = idx[:, None] - idx[None, :]
    decay = jnp.where(rel >= 0, jnp.exp(lg[:, None, None] * jnp.maximum(rel, 0.0)), 0.0)
    per_lane = lambda t: jnp.repeat(t, RET_DK, axis=1)
    gq = per_lane(jnp.exp(lg[None, :] * (idx[:, None] + 1.0)))
    wk = per_lane(jnp.exp(lg[None, :] * (chunk - 1.0 - idx[:, None])))
    gl = jnp.broadcast_to(jnp.repeat(jnp.exp(lg * chunk), RET_DK)[:, None], (RET_W, RET_W))
    head = jnp.arange(RET_W) // RET_DK
    bd = (head[:, None] == head[None, :]).astype(F32)
    gam4 = jnp.broadcast_to(jnp.exp(lg * 1.0)[None, :, None, None], (1, RET_HEADS, RET_DK, RET_DV))
    return dict(chunk=chunk, decay=decay, gq=gq, wk=wk, gl=gl, bd=bd, avg=bd / RET_DV, gam4=gam4)


def _block_diag(w):
    n, c, dd = w.shape
    out = jnp.zeros((n * c, n * dd), w.dtype)
    for i in range(n):
        out = out.at[i * c:(i + 1) * c, i * dd:(i + 1) * dd].set(w[i])
    return out


def _layer_weights(l, P):
    w_in = P["w_in"][l]
    bounds = [0, 256, 512, 768, 1024, 1280, 1408, 1440, 1696, 1952]
    kr_lo, kr_hi = bounds[6], bounds[7]
    w_in_p = jnp.concatenate(
        [w_in[:, :kr_lo], w_in[:, kr_hi:], w_in[:, kr_lo:kr_hi],
         jnp.zeros((w_in.shape[0], _N_IN_PAD - w_in.shape[1]), w_in.dtype)], axis=1)
    w_uq = P["mla_w_uq"][l]
    w_uk = P["mla_w_uk"][l]
    w_uv = P["mla_w_uv"][l]
    uk_pair = jnp.stack([_block_diag(jnp.stack([w_uk[:, 2 * p, :].T, w_uk[:, 2 * p + 1, :].T]))
                         for p in range(MLA_HEADS // 2)])
    uv_pair = jnp.stack([_block_diag(jnp.stack([w_uv[:, 2 * p, :], w_uv[:, 2 * p + 1, :]]))
                         for p in range(MLA_HEADS // 2)])
    row = lambda v: v.reshape(1, -1)
    return dict(
        norm1_w=row(P["norm1_w"][l]), w_in=w_in_p.astype(BF16),
        q_norm_w=row(P["mla_q_norm_w"][l]), kv_norm_w=row(P["mla_kv_norm_w"][l]),
        w_uq_nope=w_uq[:, :, :MLA_NOPE].reshape(MLA_Q_RANK, QN_W).astype(BF16),
        w_uq_rope=w_uq[:, :, MLA_NOPE:].reshape(MLA_Q_RANK, QR_W).astype(BF16),
        w_uk_pair=uk_pair.astype(BF16), w_uv_pair=uv_pair.astype(BF16),
        ret_gn_w=row(P["ret_gn_w"][l]),
        rg_conv_w=P["rg_conv_w"][l], rg_conv_b=row(P["rg_conv_b"][l]),
        rg_wa_bd=_block_diag(P["rg_w_a"][l]).astype(BF16), rg_b_a=row(P["rg_b_a"][l]),
        rg_wx_bd=_block_diag(P["rg_w_x"][l]).astype(BF16), rg_b_x=row(P["rg_b_x"][l]),
        rg_lambda=row(P["rg_lambda"][l]),
        w_out=P["w_out"][l].astype(BF16), norm2_w=row(P["norm2_w"][l]),
        ffn_w_up=P["ffn_w_up"][l].astype(BF16), ffn_conv_w=P["ffn_conv_w"][l],
        ffn_conv_b=row(P["ffn_conv_b"][l]), ffn_w_down=P["ffn_w_down"][l].astype(BF16),
    )


def _tile(n, pref):
    t = min(n, pref)
    assert n % t == 0, (n, pref)
    return t


def kernel(x_prompt, x_sample, cache_mla_latent, cache_mla_krope, page_table, state_ret, state_rg_conv,
           state_rglru, state_ffn_conv, norm1_w, w_in, ret_gn_w, mla_q_norm_w, mla_w_uq, mla_kv_norm_w,
           mla_w_uk, mla_w_uv, rg_conv_w, rg_conv_b, rg_w_a, rg_b_a, rg_w_x, rg_b_x, rg_lambda, w_out,
           norm2_w, ffn_w_up, ffn_conv_w, ffn_conv_b, ffn_w_down, final_norm_w):
    P = dict(norm1_w=norm1_w, w_in=w_in, ret_gn_w=ret_gn_w, mla_q_norm_w=mla_q_norm_w, mla_w_uq=mla_w_uq,
             mla_kv_norm_w=mla_kv_norm_w, mla_w_uk=mla_w_uk, mla_w_uv=mla_w_uv, rg_conv_w=rg_conv_w,
             rg_conv_b=rg_conv_b, rg_w_a=rg_w_a, rg_b_a=rg_b_a, rg_w_x=rg_w_x, rg_b_x=rg_b_x,
             rg_lambda=rg_lambda, w_out=w_out, norm2_w=norm2_w, ffn_w_up=ffn_w_up, ffn_conv_w=ffn_conv_w,
             ffn_conv_b=ffn_conv_b, ffn_w_down=ffn_w_down)
    depth = w_in.shape[0]
    B, S, D = x_prompt.shape
    DB, T, _ = x_sample.shape
    assert T == 1, "the sample path handles one new token per sequence"
    n_pages, page = page_table.shape[1], cache_mla_latent.shape[2]
    past_len = n_pages * page
    final_w = final_norm_w.reshape(1, D)
    weights = [_layer_weights(l, P) for l in range(depth)]

    chunk = RET_CHUNK if (S > RET_CHUNK and S % RET_CHUNK == 0) else S
    rt = _retention_tables(chunk)
    tabs_p = _rope_tables(jnp.arange(S, dtype=F32))
    tm_a, tt, tq, tm_e = _tile(S, 512), _tile(S, 512), _tile(S, 256), _tile(S, 512)
    ret_chunks_per_step = _tile(S // chunk, 4)
    kb = _tile(S, 512)
    x = x_prompt.reshape(B * S, D)
    p_new = []
    for l in range(depth):
        lw = weights[l]
        ret_in, qh, kcat, p_lat, p_rope, rg_in = _inproj(x, B, tm_a, lw, tabs_p)
        y_ret, sbd = _retention_prompt(ret_in, B, lw, rt, ret_chunks_per_step)
        y_rg, p_rgc, p_rgh = _rglru_prompt(rg_in, B, tt, lw)
        y_mla = _attention_prompt(qh, kcat, B, tq, kb, lw)
        x, p_ffc = _outproj_ffn(x, y_ret, y_mla, y_rg, lw, B, tm_e,
                                final_w if l == depth - 1 else None)
        s_ret = jnp.stack([sbd[:, h * RET_DK:(h + 1) * RET_DK, h * RET_DV:(h + 1) * RET_DV]
                           for h in range(RET_HEADS)], axis=1)
        p_new.append((p_lat.reshape(B, S, -1), p_rope.reshape(B, S, -1), s_ret, p_rgc,
                      p_rgh.reshape(B, -1), p_ffc))
    y_prompt = x.reshape(B, S, D)

    tabs_s = _rope_tables(jnp.full((DB,), past_len, dtype=F32))
    pages_per_chunk = _tile(n_pages, 64)
    krope_t = jnp.swapaxes(cache_mla_krope, 2, 3)
    x = x_sample.reshape(DB, D)
    s_new = []
    for l in range(depth):
        lw = weights[l]
        ret_in, qh, kcat, s_lat, s_rope, rg_in = _inproj(x, 1, DB, lw, tabs_s)
        y_ret, s_ret = _retention_step(ret_in, state_ret[l], lw, rt)
        y_rg, s_rgh = _rglru_step(rg_in, state_rg_conv[l], state_rglru[l], lw)
        o_lat = _attention_paged(qh, kcat, cache_mla_latent, krope_t, page_table, l,
                                 pages_per_chunk)
        y_mla = _uv_step(o_lat, lw)
        x, gate = _outproj_ffn(x, y_ret, y_mla, y_rg, lw, 1, DB,
                               final_w if l == depth - 1 else None, step_buf=state_ffn_conv[l])
        s_rgc = jnp.concatenate([state_rg_conv[l][:, 1:], rg_in[:, None, 0:RG_WIDTH]], axis=1)
        s_ffc = jnp.concatenate([state_ffn_conv[l][:, 1:], gate[:, None, :]], axis=1)
        s_new.append((s_lat.reshape(DB, T, -1), s_rope.reshape(DB, T, -1), s_ret, s_rgc, s_rgh, s_ffc))
    y_sample = x.reshape(DB, T, D)

    stack = lambda items, i: jnp.stack([it[i] for it in items])
    return (y_prompt, y_sample, *[stack(p_new, i) for i in range(6)], *[stack(s_new, i) for i in range(6)])
```

```python
import functools

import jax
import jax.numpy as jnp
from jax import lax
from jax.experimental import pallas as pl
from jax.experimental.pallas import tpu as pltpu

RET_HEADS = 4
RET_DK = 64
RET_DV = 64
RET_CHUNK = 128
MLA_HEADS = 8
MLA_NOPE = 64
MLA_ROPE = 32
MLA_V = 64
MLA_Q_RANK = 256
MLA_KV_RANK = 128
MLA_SCALE = (MLA_NOPE + MLA_ROPE) ** -0.5
LOG2_E = 1.4426950408889634
Q_SCALE = MLA_SCALE * LOG2_E
RG_WIDTH = 256
RG_BLOCKS = 4
RG_CONV = 4
RG_C = 8.0
FFN_CONV = 3
ROPE_BASE = 10000.0
EPS = 1e-6
NEG_INF = -1e30

RET_W = RET_HEADS * RET_DK
QR_W = MLA_HEADS * MLA_ROPE
QN_W = MLA_HEADS * MLA_NOPE
KCAT_W = 256
LANE = 128
VMEM_LIMIT = 56 * 1024 * 1024
PAGED_SLOTS = 3

BF16 = jnp.bfloat16
F32 = jnp.float32

_OFF_RQ, _OFF_RK, _OFF_RV, _OFF_RG = 0, 256, 512, 768
_OFF_CQ, _OFF_CKV, _OFF_GX, _OFF_GG, _OFF_KR = 1024, 1280, 1408, 1664, 1920
_N_IN_PAD = 2048


def _params(n_axes):
    return pltpu.CompilerParams(dimension_semantics=("arbitrary",) * n_axes,
                                vmem_limit_bytes=VMEM_LIMIT)


def _const_spec(shape):
    n = len(shape)
    return pl.BlockSpec(shape, lambda *_: (0,) * n, pipeline_mode=pl.Buffered(1))


def _dot(a, b):
    return jnp.dot(a, b, preferred_element_type=F32)


def _dot_nt(a, b):
    return lax.dot_general(a, b, (((1,), (1,)), ((), ())), preferred_element_type=F32)


def _dot_tn(a, b):
    return lax.dot_general(a, b, (((0,), (0,)), ((), ())), preferred_element_type=F32)


def _rms(x, w):
    return x * lax.rsqrt(jnp.mean(x * x, axis=-1, keepdims=True) + EPS) * w


def _rope3(x, tab_ref, half):
    w = x.shape[-1]
    return (x * tab_ref[0] + pltpu.roll(x, w - half, 1) * tab_ref[1]
            + pltpu.roll(x, half, 1) * tab_ref[2])


def _inproj_kernel(x_ref, n1_ref, win_ref, tabr_ref, tabq_ref, tabk_ref, qnw_ref, wuqn_ref,
                   wuqr_ref, wukp_ref, kvnw_ref,
                   ret_ref, q_ref, kcat_ref, vt_ref, plat_ref, prope_ref, rg_ref):
    h = _rms(x_ref[...], n1_ref[...]).astype(BF16)
    z = _dot(h, win_ref[...])

    ret_ref[:, _OFF_RQ:_OFF_RQ + RET_W] = _rope3(z[:, _OFF_RQ:_OFF_RQ + RET_W], tabr_ref, RET_DK // 2)
    ret_ref[:, _OFF_RK:_OFF_RK + RET_W] = (
        _rope3(z[:, _OFF_RK:_OFF_RK + RET_W], tabr_ref, RET_DK // 2) * (RET_DK ** -0.5))
    ret_ref[:, _OFF_RV:_OFF_CQ] = z[:, _OFF_RV:_OFF_CQ]
    rg_ref[...] = z[:, _OFF_GX:_OFF_KR]

    ckvn = _rms(z[:, _OFF_CKV:_OFF_CKV + MLA_KV_RANK], kvnw_ref[...])
    plat_ref[...] = ckvn
    krr = _rope3(z[:, _OFF_KR:_OFF_KR + LANE], tabk_ref, MLA_ROPE // 2)
    prope_ref[...] = krr[:, :MLA_ROPE]
    kcat_ref[:, 0:MLA_KV_RANK] = ckvn.astype(BF16)
    kcat_ref[:, MLA_KV_RANK:KCAT_W] = krr.astype(BF16)
    vt_ref[...] = ckvn.T.astype(BF16)

    cqn = _rms(z[:, _OFF_CQ:_OFF_CQ + MLA_Q_RANK], qnw_ref[...]).astype(BF16)
    qn = _dot(cqn, wuqn_ref[...]).astype(BF16)
    qr = _rope3(_dot(cqn, wuqr_ref[...]), tabq_ref, MLA_ROPE // 2) * Q_SCALE
    lane = lax.broadcasted_iota(jnp.int32, (1, LANE), 1)
    heads_per_tile = LANE // MLA_ROPE
    for p in range(MLA_HEADS // 2):
        ql = _dot(qn[:, p * LANE:(p + 1) * LANE], wukp_ref[p]) * Q_SCALE
        for e in range(2):
            hd = 2 * p + e
            q_ref[hd, :, 0:MLA_KV_RANK] = ql[:, e * LANE:(e + 1) * LANE].astype(BF16)
            tile = qr[:, (hd // heads_per_tile) * LANE:(hd // heads_per_tile + 1) * LANE]
            sh = MLA_ROPE * (hd % heads_per_tile)
            if sh:
                tile = pltpu.roll(tile, LANE - sh, 1)
            q_ref[hd, :, MLA_KV_RANK:KCAT_W] = jnp.where(lane < MLA_ROPE, tile, 0.0).astype(BF16)


def _inproj(x2d, n_batch, tm, lw, tabs):
    rows, d = x2d.shape
    ns = rows // (n_batch * tm)
    tabr, tabq, tabk = tabs
    row_map = lambda s, b: (b * ns + s, 0)
    tab_map = lambda s, b: (0, s, 0)
    out_shape = (
        jax.ShapeDtypeStruct((rows, 4 * RET_W), F32),
        jax.ShapeDtypeStruct((MLA_HEADS, rows, KCAT_W), BF16),
        jax.ShapeDtypeStruct((rows, KCAT_W), BF16),
        jax.ShapeDtypeStruct((MLA_KV_RANK, rows), BF16),
        jax.ShapeDtypeStruct((rows, MLA_KV_RANK), F32),
        jax.ShapeDtypeStruct((rows, MLA_ROPE), F32),
        jax.ShapeDtypeStruct((rows, 2 * RG_WIDTH), F32),
    )
    return pl.pallas_call(
        _inproj_kernel,
        grid=(ns, n_batch),
        in_specs=[
            pl.BlockSpec((tm, d), row_map),
            _const_spec((1, d)),
            _const_spec((d, _N_IN_PAD)),
            pl.BlockSpec((3, tm, RET_W), tab_map),
            pl.BlockSpec((3, tm, QR_W), tab_map),
            pl.BlockSpec((3, tm, LANE), tab_map),
            _const_spec((1, MLA_Q_RANK)),
            _const_spec((MLA_Q_RANK, QN_W)),
            _const_spec((MLA_Q_RANK, QR_W)),
            _const_spec((MLA_HEADS // 2, LANE, 2 * MLA_KV_RANK)),
            _const_spec((1, MLA_KV_RANK)),
        ],
        out_specs=[
            pl.BlockSpec((tm, 4 * RET_W), row_map),
            pl.BlockSpec((MLA_HEADS, tm, KCAT_W), lambda s, b: (0, b * ns + s, 0)),
            pl.BlockSpec((tm, KCAT_W), row_map),
            pl.BlockSpec((MLA_KV_RANK, tm), lambda s, b: (0, b * ns + s)),
            pl.BlockSpec((tm, MLA_KV_RANK), row_map),
            pl.BlockSpec((tm, MLA_ROPE), row_map),
            pl.BlockSpec((tm, 2 * RG_WIDTH), row_map),
        ],
        out_shape=out_shape,
        compiler_params=_params(2),
        name="inproj",
    )(x2d, lw["norm1_w"], lw["w_in"], tabr, tabq, tabk, lw["q_norm_w"], lw["w_uq_nope"],
      lw["w_uq_rope"], lw["w_uk_pair"], lw["kv_norm_w"])


def _group_norm_gate(o, g, gnw, avg):
    mu = _dot(o, avg)
    d = o - mu
    var = _dot(d * d, avg)
    return d * lax.rsqrt(var + EPS) * gnw * (g * jax.nn.sigmoid(g))


def _ret_kernel(ret_ref, dec_ref, gq_ref, wk_ref, gl_ref, bd_ref, gnw_ref, avg_ref,
                y_ref, sbd_ref, s_scr):
    @pl.when(pl.program_id(1) == 0)
    def _():
        s_scr[...] = jnp.zeros_like(s_scr)

    L = dec_ref.shape[1]
    head_of_lane = lax.broadcasted_iota(jnp.int32, (1, RET_W), 1) // RET_DK
    s_old = s_scr[...]
    for cc in range(ret_ref.shape[0] // L):
        rows = pl.ds(cc * L, L)
        q = ret_ref[rows, _OFF_RQ:_OFF_RQ + RET_W]
        k = ret_ref[rows, _OFF_RK:_OFF_RK + RET_W]
        vb = ret_ref[rows, _OFF_RV:_OFF_RV + RET_W].astype(BF16)
        g = ret_ref[rows, _OFF_RG:_OFF_RG + RET_W]
        kb = k.astype(BF16)
        o = _dot(q.astype(BF16), s_old.astype(BF16)) * gq_ref[...]
        for h in range(RET_HEADS):
            mh = head_of_lane == h
            s = _dot_nt(jnp.where(mh, q, 0.0).astype(BF16), kb) * dec_ref[h]
            o = o + jnp.where(mh, _dot(s.astype(BF16), vb), 0.0)
        upd = _dot_tn((k * wk_ref[...]).astype(BF16), vb)
        s_old = gl_ref[...] * s_old + bd_ref[...] * upd
        y_ref[rows, :] = _group_norm_gate(o, g, gnw_ref[...], avg_ref[...])
    s_scr[...] = s_old
    sbd_ref[...] = s_old


def _retention_prompt(ret_in, n_batch, lw, rt, chunks_per_step):
    rows = ret_in.shape[0]
    L = rt["chunk"]
    tb = L * chunks_per_step
    nc = rows // (n_batch * tb)
    return pl.pallas_call(
        _ret_kernel,
        grid=(n_batch, nc),
        in_specs=[
            pl.BlockSpec((tb, 4 * RET_W), lambda b, c: (b * nc + c, 0)),
            _const_spec((RET_HEADS, L, L)),
            _const_spec((L, RET_W)),
            _const_spec((L, RET_W)),
            _const_spec((RET_W, RET_W)),
            _const_spec((RET_W, RET_W)),
            _const_spec((1, RET_W)),
            _const_spec((RET_W, RET_W)),
        ],
        out_specs=[
            pl.BlockSpec((tb, RET_W), lambda b, c: (b * nc + c, 0)),
            pl.BlockSpec((None, RET_W, RET_W), lambda b, c: (b, 0, 0)),
        ],
        out_shape=(jax.ShapeDtypeStruct((rows, RET_W), F32),
                   jax.ShapeDtypeStruct((n_batch, RET_W, RET_W), F32)),
        scratch_shapes=[pltpu.VMEM((RET_W, RET_W), F32)],
        compiler_params=_params(2),
        name="retention",
    )(ret_in, rt["decay"], rt["gq"], rt["wk"], rt["gl"], rt["bd"], lw["ret_gn_w"], rt["avg"])


def _ret_step_kernel(q_ref, k_ref, v_ref, g_ref, s0_ref, gam_ref, gnw_ref, y_ref, s_ref):
    s_new = gam_ref[...] * s0_ref[...] + k_ref[...] * v_ref[...]
    s_ref[...] = s_new
    o = jnp.sum(q_ref[...] * s_new, axis=2, keepdims=True)
    mu = jnp.mean(o, axis=3, keepdims=True)
    d = o - mu
    var = jnp.mean(d * d, axis=3, keepdims=True)
    g = g_ref[...]
    y_ref[...] = d * lax.rsqrt(var + EPS) * gnw_ref[...] * (g * jax.nn.sigmoid(g))


def _retention_step(ret_in, s0, lw, rt):
    n = ret_in.shape[0]
    bs = _tile(n, 8)
    q4 =ret_in[:, _OFF_RQ:_OFF_RQ + RET_W].reshape(n, RET_HEADS, RET_DK, 1)
    k4 = ret_in[:, _OFF_RK:_OFF_RK + RET_W].reshape(n, RET_HEADS, RET_DK, 1)
    v4 = ret_in[:, _OFF_RV:_OFF_RV + RET_W].reshape(n, RET_HEADS, 1, RET_DV)
    g4 = ret_in[:, _OFF_RG:_OFF_RG + RET_W].reshape(n, RET_HEADS, 1, RET_DV)
    col = pl.BlockSpec((bs, RET_HEADS, RET_DK, 1), lambda i: (i, 0, 0, 0))
    row = pl.BlockSpec((bs, RET_HEADS, 1, RET_DV), lambda i: (i, 0, 0, 0))
    mat = pl.BlockSpec((bs, RET_HEADS, RET_DK, RET_DV), lambda i: (i, 0, 0, 0))
    y4, s_new = pl.pallas_call(
        _ret_step_kernel,
        grid=(n // bs,),
        in_specs=[col, col, row, row, mat,
                  _const_spec((1, RET_HEADS, RET_DK, RET_DV)),
                  _const_spec((1, RET_HEADS, 1, RET_DV))],
        out_specs=[row, mat],
        out_shape=(jax.ShapeDtypeStruct((n, RET_HEADS, 1, RET_DV), F32),
                   jax.ShapeDtypeStruct((n, RET_HEADS, RET_DK, RET_DV), F32)),
        compiler_params=_params(1),
        name="retention_step",
    )(q4, k4, v4, g4, s0, rt["gam4"], lw["ret_gn_w"].reshape(1, RET_HEADS, 1, RET_DV))
    return y4.reshape(n, RET_W), s_new


def _rg_gates(xc, wa_ref, ba_ref, wx_ref, bx_ref, lam_ref):
    xb = xc.astype(BF16)
    r = jax.nn.sigmoid(_dot(xb, wa_ref[...]) + ba_ref[...])
    i = jax.nn.sigmoid(_dot(xb, wx_ref[...]) + bx_ref[...])
    y = -lam_ref[...]
    softplus = jnp.maximum(y, 0.0) + jnp.log1p(jnp.exp(-jnp.abs(y)))
    log_a = -RG_C * r * softplus
    a = jnp.exp(log_a)
    u = jnp.sqrt(1.0 - jnp.exp(2.0 * log_a)) * i * xc
    return a, u


def _rg_kernel(rg_ref, cw_ref, cb_ref, wa_ref, ba_ref, wx_ref, bx_ref, lam_ref,
               y_ref, buf_ref, hlast_ref, xs_scr, h_scr):
    tt = rg_ref.shape[0]
    halo = 8

    @pl.when(pl.program_id(1) == 0)
    def _():
        xs_scr[0:halo, :] = jnp.zeros((halo, RG_WIDTH), F32)
        h_scr[...] = jnp.zeros_like(h_scr)

    gx = rg_ref[:, 0:RG_WIDTH]
    gg = rg_ref[:, RG_WIDTH:2 * RG_WIDTH]
    xs_scr[halo:halo + tt, :] = gx
    xc = cb_ref[...] + gx * cw_ref[RG_CONV - 1:RG_CONV, :]
    for j in range(1, RG_CONV):
        xc = xc + xs_scr[halo - j:halo - j + tt, :] * cw_ref[RG_CONV - 1 - j:RG_CONV - j, :]
    xs_scr[0:halo, :] = xs_scr[tt:tt + halo, :]
    buf_ref[...] = gx[tt - (RG_CONV - 1):tt, :]

    a, u = _rg_gates(xc, wa_ref, ba_ref, wx_ref, bx_ref, lam_ref)
    row = lax.broadcasted_iota(jnp.int32, (tt, 1), 0)
    step = 1
    while step < tt:
        keep = row >= step
        a_sh = jnp.where(keep, pltpu.roll(a, step, 0), 1.0)
        u_sh = jnp.where(keep, pltpu.roll(u, step, 0), 0.0)
        u = a * u_sh + u
        a = a * a_sh
        step *= 2
    h = u + a * h_scr[...]
    h_scr[...] = h[tt - 1:tt, :]
    hlast_ref[...] = h[tt - 1:tt, :]
    y_ref[...] = h * jax.nn.gelu(gg, approximate=True)


def _rglru_prompt(rg_in, n_batch, tt, lw):
    rows = rg_in.shape[0]
    nt = rows // (n_batch * tt)
    w = RG_WIDTH
    return pl.pallas_call(
        _rg_kernel,
        grid=(n_batch, nt),
        in_specs=[
            pl.BlockSpec((tt, 2 * w), lambda b, t: (b * nt + t, 0)),
            _const_spec((RG_CONV, w)), _const_spec((1, w)),
            _const_spec((w, w)), _const_spec((1, w)),
            _const_spec((w, w)), _const_spec((1, w)), _const_spec((1, w)),
        ],
        out_specs=[
            pl.BlockSpec((tt, w), lambda b, t: (b * nt + t, 0)),
            pl.BlockSpec((None, RG_CONV - 1, w), lambda b, t: (b, 0, 0)),
            pl.BlockSpec((None, 1, w), lambda b, t: (b, 0, 0)),
        ],
        out_shape=(jax.ShapeDtypeStruct((rows, w), F32),
                   jax.ShapeDtypeStruct((n_batch, RG_CONV - 1, w), F32),
                   jax.ShapeDtypeStruct((n_batch, 1, w), F32)),
        scratch_shapes=[pltpu.VMEM((tt + 8, w), F32), pltpu.VMEM((1, w), F32)],
        compiler_params=_params(2),
        name="rglru",
    )(rg_in, lw["rg_conv_w"], lw["rg_conv_b"], lw["rg_wa_bd"], lw["rg_b_a"], lw["rg_wx_bd"],
      lw["rg_b_x"], lw["rg_lambda"])


def _rg_step_kernel(rg_ref, b0_ref, b1_ref, b2_ref, h0_ref, cw_ref, cb_ref, wa_ref, ba_ref,
                    wx_ref, bx_ref, lam_ref, y_ref, h_ref):
    gx = rg_ref[:, 0:RG_WIDTH]
    gg = rg_ref[:, RG_WIDTH:2 * RG_WIDTH]
    xc = (cb_ref[...] + b0_ref[...] * cw_ref[0:1, :] + b1_ref[...] * cw_ref[1:2, :]
          + b2_ref[...] * cw_ref[2:3, :] + gx * cw_ref[3:4, :])
    a, u = _rg_gates(xc, wa_ref, ba_ref, wx_ref, bx_ref, lam_ref)
    h = u + a * h0_ref[...]
    h_ref[...] = h
    y_ref[...] = h * jax.nn.gelu(gg, approximate=True)


def _rglru_step(rg_in, buf0, h0, lw):
    n = rg_in.shape[0]
    w = RG_WIDTH
    full = lambda shape: pl.BlockSpec(shape, lambda i: (0,) * len(shape))
    return pl.pallas_call(
        _rg_step_kernel,
        grid=(1,),
        in_specs=[full((n, 2 * w)), full((n, w)), full((n, w)), full((n, w)), full((n, w)),
                  full((RG_CONV, w)), full((1, w)), full((w, w)), full((1, w)), full((w, w)),
                  full((1, w)), full((1, w))],
        out_specs=[full((n, w)), full((n, w))],
        out_shape=(jax.ShapeDtypeStruct((n, w), F32), jax.ShapeDtypeStruct((n, w), F32)),
        compiler_params=_params(1),
        name="rglru_step",
    )(rg_in, buf0[:, 0], buf0[:, 1], buf0[:, 2], h0, lw["rg_conv_w"], lw["rg_conv_b"],
      lw["rg_wa_bd"], lw["rg_b_a"], lw["rg_wx_bd"], lw["rg_b_x"], lw["rg_lambda"])


def _uv_project_t(o_t, tq, wuv_ref, out_ref):
    for p in range(MLA_HEADS // 2):
        pair_t = jnp.concatenate([o_t[:, (2 * p) * tq:(2 * p + 1) * tq],
                                  o_t[:, (2 * p + 1) * tq:(2 * p + 2) * tq]], axis=0).astype(BF16)
        out_ref[:, p * LANE:(p + 1) * LANE] = _dot_tn(pair_t, wuv_ref[p])


def _attn_kernel(q_ref, k_ref, vt_ref, wuv_ref, o_ref, sa_scr, sb_scr, m_scr, l_scr, acc_scr, *, kb):
    tq = q_ref.shape[1]
    assert tq & (tq - 1) == 0 and kb % tq == 0
    m_cols = MLA_HEADS * tq
    i = pl.program_id(1)
    q = q_ref[...].reshape(m_cols, KCAT_W)
    m_scr[...] = jnp.full_like(m_scr, NEG_INF)
    l_scr[...] = jnp.zeros_like(l_scr)
    acc_scr[...] = jnp.zeros_like(acc_scr)

    def scores(dst, j):
        dst[...] = _dot_nt(k_ref[pl.ds(pl.multiple_of(j * kb, kb), kb), :], q)

    def absorb(src, j, masked):
        s = src[...]
        if masked:
            krow = lax.broadcasted_iota(jnp.int32, (kb, m_cols), 0) + (j * kb - i * tq)
            qcol = lax.broadcasted_iota(jnp.int32, (kb, m_cols), 1) & (tq - 1)
            s = jnp.where(krow <= qcol, s, NEG_INF)
        m_prev = m_scr[...]
        m_next = jnp.maximum(m_prev, jnp.max(s, axis=0, keepdims=True))
        p = jnp.exp2(s - m_next)
        alpha = jnp.exp2(m_prev - m_next)
        l_scr[...] = alpha * l_scr[...] + jnp.sum(p, axis=0, keepdims=True)
        vt = vt_ref[:, pl.ds(pl.multiple_of(j * kb, kb), kb)]
        acc_scr[...] = alpha * acc_scr[...] + _dot(vt, p.astype(BF16))
        m_scr[...] = m_next

    n_blocks = (i * tq) // kb + 1
    n_pairs = (n_blocks - 1) // 2
    scores(sa_scr, 0)

    def pair(t, carry):
        scores(sb_scr, 2 * t + 1)
        absorb(sa_scr, 2 * t, False)
        scores(sa_scr, 2 * t + 2)
        absorb(sb_scr, 2 * t + 1, False)
        return carry

    lax.fori_loop(0, n_pairs, pair, 0)
    odd_left = (n_blocks - 1) - 2 * n_pairs

    @pl.when(odd_left == 0)
    def _():
        absorb(sa_scr, n_blocks - 1, True)

    @pl.when(odd_left == 1)
    def _():
        scores(sb_scr, n_blocks - 1)
        absorb(sa_scr, n_blocks - 2, False)
        absorb(sb_scr, n_blocks - 1, True)

    o_t = acc_scr[...] / l_scr[...]
    _uv_project_t(o_t, tq, wuv_ref, o_ref)


def _attention_prompt(qh, kcat, vt, n_batch, tq, kb, lw):
    rows = kcat.shape[0]
    s_len = rows // n_batch
    nq = s_len // tq
    m_cols = MLA_HEADS * tq
    return pl.pallas_call(
        functools.partial(_attn_kernel, kb=kb),
        grid=(n_batch, nq),
        in_specs=[
            pl.BlockSpec((MLA_HEADS, tq, KCAT_W), lambda b, i: (0, b * nq + i, 0)),
            pl.BlockSpec((s_len, KCAT_W), lambda b, i: (b, 0)),
            pl.BlockSpec((MLA_KV_RANK, s_len), lambda b, i: (0, b)),
            _const_spec((MLA_HEADS // 2, 2 * MLA_KV_RANK, LANE)),
        ],
        out_specs=pl.BlockSpec((tq, MLA_HEADS * MLA_V), lambda b, i: (b * nq + i, 0)),
        out_shape=jax.ShapeDtypeStruct((rows, MLA_HEADS * MLA_V), F32),
        scratch_shapes=[pltpu.VMEM((kb, m_cols), F32), pltpu.VMEM((kb, m_cols), F32),
                        pltpu.VMEM((1, m_cols), F32), pltpu.VMEM((1, m_cols), F32),
                        pltpu.VMEM((MLA_KV_RANK, m_cols), F32)],
        compiler_params=_params(2),
        name="attention",
    )(qh, kcat, vt, lw["w_uv_pair"])


def _paged_kernel(pt_ref, q_ref, kself_ref, lat_hbm, krt_hbm, o_ref,
                  lat_buf, krt_buf, sem, m_scr, l_scr, acc_scr, *, layer, pages_per_chunk,
                  chunks_per_seq, sub_keys):
    g = pl.program_id(0)
    page = lat_hbm.shape[2]
    chunk_keys = pages_per_chunk * page
    n_slots = lat_buf.shape[0]
    ahead = n_slots - 1

    last = pl.num_programs(0) - 1
    slot = g % n_slots
    fill = (g + ahead) % n_slots
    n_sub = chunk_keys // sub_keys
    pages_per_sub = pages_per_chunk // n_sub

    def page_copies(pid, dst_slot, j):
        cols = pl.ds(j * page, page)
        return (pltpu.make_async_copy(lat_hbm.at[layer, pid], lat_buf.at[dst_slot, cols, :], sem.at[dst_slot, 0]),
                pltpu.make_async_copy(krt_hbm.at[layer, pid], krt_buf.at[dst_slot, :, cols], sem.at[dst_slot, 1]))

    def start_pages(step, dst_slot, pages):
        for j in pages:
            for cp in page_copies(pt_ref[step * pages_per_chunk + j], dst_slot, j):
                cp.start()

    def wait_chunk(dst_slot):
        for j in range(pages_per_chunk):
            for cp in page_copies(0, dst_slot, j):
                cp.wait()

    @pl.when(g == 0)
    def _():
        for a in range(ahead):
            start_pages(jnp.minimum(a, last), a, range(pages_per_chunk))

    wait_chunk(slot)
    c = g % chunks_per_seq

    @pl.when(c == 0)
    def _():
        m_scr[...] = jnp.full_like(m_scr, NEG_INF)
        l_scr[...] = jnp.zeros_like(l_scr)
        acc_scr[...] = jnp.zeros_like(acc_scr)

    q = q_ref[...]
    q_lat = q[:, 0:MLA_KV_RANK]
    q_rope = q[:, MLA_KV_RANK:MLA_KV_RANK + MLA_ROPE]
    nxt = jnp.minimum(g + ahead, last)
    lats, scores = [], []
    for sb in range(n_sub):
        start_pages(nxt, fill, range(sb * pages_per_sub, (sb + 1) * pages_per_sub))
        keys = pl.ds(sb * sub_keys, sub_keys)
        lat = lat_buf[slot, keys, :].astype(BF16)
        krt = krt_buf[slot, :, keys].astype(BF16)
        lats.append(lat)
        scores.append(_dot_nt(q_lat, lat) + _dot(q_rope, krt))
    m_prev = m_scr[...]
    m_cur = functools.reduce(jnp.maximum, [jnp.max(s, axis=1, keepdims=True) for s in scores])
    m_next = jnp.maximum(m_prev, m_cur)
    alpha = jnp.exp2(m_prev - m_next)
    l_new = alpha * l_scr[...]
    acc = alpha * acc_scr[...]
    for s, lat in zip(scores, lats):
        p = jnp.exp2(s - m_next)
        l_new = l_new + jnp.sum(p, axis=1, keepdims=True)
        acc = acc + _dot(p.astype(BF16), lat)
    m_scr[...] = m_next
    l_scr[...] = l_new
    acc_scr[...] = acc

    @pl.when(g == last)
    def _():
        for a in range(1, n_slots):
            wait_chunk((g + a) % n_slots)

    @pl.when(c == chunks_per_seq - 1)
    def _():
        kself = kself_ref[...].astype(F32)
        s_self = jnp.sum(q.astype(F32) * kself, axis=1, keepdims=True)
        m_prev = m_scr[...]
        m_fin = jnp.maximum(m_prev, s_self)
        alpha = jnp.exp2(m_prev - m_fin)
        p_self = jnp.exp2(s_self - m_fin)
        l_fin = alpha * l_scr[...] + p_self
        acc_fin = alpha * acc_scr[...] + p_self * kself[:, 0:MLA_KV_RANK]
        o_ref[...] = acc_fin / l_fin


def _attention_paged(qh, kcat, pool_lat, pool_rope_t, page_table, layer, pages_per_chunk):
    n = kcat.shape[0]
    n_pages = page_table.shape[1]
    page, rank = pool_lat.shape[2], pool_lat.shape[3]
    rope = pool_rope_t.shape[2]
    ppc = pages_per_chunk
    cps = n_pages // ppc
    chunk_keys = ppc * page
    q3 = jnp.transpose(qh, (1, 0, 2))
    kself = kcat.reshape(n, 1, KCAT_W)
    pt_flat = page_table.reshape(-1)
    seq_map = lambda g, pt: (g // cps, 0, 0)
    grid_spec = pltpu.PrefetchScalarGridSpec(
        num_scalar_prefetch=1,
        grid=(n * cps,),
        in_specs=[pl.BlockSpec((None, MLA_HEADS, KCAT_W), seq_map),
                  pl.BlockSpec((None, 1, KCAT_W), seq_map),
                  pl.BlockSpec(memory_space=pl.ANY),
                  pl.BlockSpec(memory_space=pl.ANY)],
        out_specs=pl.BlockSpec((None, MLA_HEADS, MLA_KV_RANK), seq_map),
        scratch_shapes=[pltpu.VMEM((PAGED_SLOTS, chunk_keys, rank), F32),
                        pltpu.VMEM((PAGED_SLOTS, rope, chunk_keys), F32),
                        pltpu.SemaphoreType.DMA((PAGED_SLOTS, 2)),
                        pltpu.VMEM((MLA_HEADS, 1), F32), pltpu.VMEM((MLA_HEADS, 1), F32),
                        pltpu.VMEM((MLA_HEADS, MLA_KV_RANK), F32)],
    )
    return pl.pallas_call(
        functools.partial(_paged_kernel, layer=layer, pages_per_chunk=ppc, chunks_per_seq=cps,
                          sub_keys=min(chunk_keys, 2048)),
        grid_spec=grid_spec,
        out_shape=jax.ShapeDtypeStruct((n, MLA_HEADS, MLA_KV_RANK), F32),
        compiler_params=_params(1),
        name="attention_paged",
    )(pt_flat, q3, kself, pool_lat, pool_rope_t)


def _uv_kernel(o_ref, wuv_ref, y_ref):
    for p in range(MLA_HEADS // 2):
        pair = o_ref[:, 2 * p * MLA_KV_RANK:(2 * p + 2) * MLA_KV_RANK].astype(BF16)
        y_ref[:, p * LANE:(p + 1) * LANE] = _dot(pair, wuv_ref[p])


def _uv_step(o_lat, lw):
    n = o_lat.shape[0]
    o2 = o_lat.reshape(n, MLA_HEADS * MLA_KV_RANK)
    full = lambda shape: pl.BlockSpec(shape, lambda i: (0,) * len(shape))
    return pl.pallas_call(
        _uv_kernel,
        grid=(1,),
        in_specs=[full(o2.shape), full((MLA_HEADS // 2, 2 * MLA_KV_RANK, LANE))],
        out_specs=full((n, MLA_HEADS * MLA_V)),
        out_shape=jax.ShapeDtypeStruct((n, MLA_HEADS * MLA_V), F32),
        compiler_params=_params(1),
        name="uv_step",
    )(o2, lw["w_uv_pair"])


def _ff_bounds(d_ff, max_tiles=4, tile=256):
    if d_ff % tile:
        return (0, d_ff)
    tiles = d_ff // tile
    n = -(-tiles // max_tiles)
    sizes = [tiles // n + (1 if c >= n - tiles % n else 0) for c in range(n)]
    bounds = [0]
    for s in sizes:
        bounds.append(bounds[-1] + s * tile)
    return tuple(bounds)


def _ffn_kernel(*refs, stepwise, final_norm, bounds):
    x_ref, yret_ref, ymla_ref, yrg_ref, wout_ref, n2_ref, wup_ref, cw_ref, cb_ref, wdn_ref = refs[:10]
    pos = 10
    fn_ref = None
    if final_norm:
        fn_ref = refs[pos]
        pos += 1
    if stepwise:
        bm2_ref, bm1_ref = refs[pos:pos + 2]
        out_ref, gate_ref = refs[pos + 2:pos + 4]
    else:
        out_ref, tail_ref, g_scr, carry_scr = refs[pos:pos + 4]
    tm = x_ref.shape[0]
    d_ff = wdn_ref.shape[0]
    halo = 8

    if not stepwise:
        @pl.when(pl.program_id(1) == 0)
        def _():
            carry_scr[...] = jnp.zeros_like(carry_scr)

    mix = jnp.concatenate([yret_ref[...], ymla_ref[...], yrg_ref[...]], axis=1).astype(BF16)
    x1 = x_ref[...] + _dot(mix, wout_ref[...])
    h2 = _rms(x1, n2_ref[...]).astype(BF16)
    acc = jnp.zeros_like(x1)
    for c in range(len(bounds) - 1):
        lo, hi = bounds[c], bounds[c + 1]
        tf = hi - lo
        gate = _dot(h2, wup_ref[:, lo:hi])
        val = _dot(h2, wup_ref[:, d_ff + lo:d_ff + hi])
        if stepwise:
            gm2 = bm2_ref[:, lo:hi]
            gm1 = bm1_ref[:, lo:hi]
            gate_ref[:, lo:hi] = gate
        else:
            g_scr[0:halo, 0:tf] = carry_scr[c, :, 0:tf]
            g_scr[halo:halo + tm, 0:tf] = gate
            gm1 = g_scr[halo - 1:halo - 1 + tm, 0:tf]
            gm2 = g_scr[halo - 2:halo - 2 + tm, 0:tf]
            carry_scr[c, :, 0:tf] = g_scr[tm:tm + halo, 0:tf]
            tail_ref[:, lo:hi] = gate[tm - (FFN_CONV - 1):tm, :]
        gc = (cb_ref[:, lo:hi] + gm2 * cw_ref[0:1, lo:hi] + gm1 * cw_ref[1:2, lo:hi]
              + gate * cw_ref[2:3, lo:hi])
        act = (gc * jax.nn.sigmoid(gc) * val).astype(BF16)
        acc = acc + _dot(act, wdn_ref[lo:hi, :])
    x2 = x1 + acc
    out_ref[...] = _rms(x2, fn_ref[...]) if final_norm else x2


def _outproj_ffn(x2d, y_ret, y_mla, y_rg, lw, n_batch, tm, final_w, step_buf=None):
    rows, d = x2d.shape
    d_ff = lw["ffn_w_down"].shape[0]
    bounds = _ff_bounds(d_ff)
    n_chunks = len(bounds) - 1
    tf = max(bounds[c + 1] - bounds[c] for c in range(n_chunks))
    stepwise = step_buf is not None
    final_norm = final_w is not None
    nt = rows // (n_batch * tm)
    row_map = lambda b, t: (b * nt + t, 0)
    in_specs = [
        pl.BlockSpec((tm, d), row_map),
        pl.BlockSpec((tm, RET_W), row_map),
        pl.BlockSpec((tm, MLA_HEADS * MLA_V), row_map),
        pl.BlockSpec((tm, RG_WIDTH), row_map),
        _const_spec((d, d)), _const_spec((1, d)), _const_spec((d, 2 * d_ff)),
        _const_spec((FFN_CONV, d_ff)), _const_spec((1, d_ff)), _const_spec((d_ff, d)),
    ]
    args = [x2d, y_ret, y_mla, y_rg, lw["w_out"], lw["norm2_w"], lw["ffn_w_up"], lw["ffn_conv_w"],
            lw["ffn_conv_b"], lw["ffn_w_down"]]
    if final_norm:
        in_specs.append(_const_spec((1, d)))
        args.append(final_w)
    if stepwise:
        in_specs += [pl.BlockSpec((tm, d_ff), row_map)] * 2
        args += [step_buf[:, 0], step_buf[:, 1]]
        out_specs = [pl.BlockSpec((tm, d), row_map), pl.BlockSpec((tm, d_ff), row_map)]
        out_shape = (jax.ShapeDtypeStruct((rows, d), F32), jax.ShapeDtypeStruct((rows, d_ff), F32))
        scratch = []
    else:
        out_specs = [pl.BlockSpec((tm, d), row_map),
                     pl.BlockSpec((None, FFN_CONV - 1, d_ff), lambda b, t: (b, 0, 0))]
        out_shape = (jax.ShapeDtypeStruct((rows, d), F32),
                     jax.ShapeDtypeStruct((n_batch, FFN_CONV - 1, d_ff), F32))
        scratch = [pltpu.VMEM((tm + 8, tf), F32), pltpu.VMEM((n_chunks, 8, tf), F32)]
    return pl.pallas_call(
        functools.partial(_ffn_kernel, stepwise=stepwise, final_norm=final_norm, bounds=bounds),
        grid=(n_batch, nt),
        in_specs=in_specs,
        out_specs=out_specs,
        out_shape=out_shape,
        scratch_shapes=scratch,
        compiler_params=_params(2),
        name="outproj_ffn_step" if stepwise else "outproj_ffn",
    )(*args)


def _rope_tables(pos):
    def cos_sin(half):
        freqs = ROPE_BASE ** (-jnp.arange(half, dtype=F32) / half)
        ang = pos[:, None] * freqs[None, :]
        return jnp.cos(ang), jnp.sin(ang)

    def three(cos, sin, reps, width):
        zero = jnp.zeros_like(sin)
        tabs = [jnp.concatenate([cos, cos], 1), jnp.concatenate([-sin, zero], 1),
                jnp.concatenate([zero, sin], 1)]
        tabs = [jnp.tile(t, (1, reps)) for t in tabs]
        tabs = [jnp.pad(t, ((0, 0), (0, width - t.shape[1]))) for t in tabs]
        return jnp.stack(tabs)

    cos_r, sin_r = cos_sin(RET_DK // 2)
    cos_m, sin_m = cos_sin(MLA_ROPE // 2)
    return (three(cos_r, sin_r, RET_HEADS, RET_W), three(cos_m, sin_m, MLA_HEADS, QR_W),
            three(cos_m, sin_m, 1, LANE))


def _retention_tables(chunk):
    lg = jnp.log(1.0 - 2.0 ** (-5.0 - jnp.arange(RET_HEADS, dtype=F32)))
    idx = jnp.arange(chunk, dtype=F32)
    rel = idx[:, None] - idx[None, :]
    decay = jnp.where(rel >= 0, jnp.exp(lg[:, None, None] * jnp.maximum(rel, 0.0)), 0.0)
    per_lane = lambda t: jnp.repeat(t, RET_DK, axis=1)
    gq = per_lane(jnp.exp(lg[None, :] * (idx[:, None] + 1.0)))
    wk = per_lane(jnp.exp(lg[None, :] * (chunk - 1.0 - idx[:, None])))
    gl = jnp.broadcast_to(jnp.repeat(jnp.exp(lg * chunk), RET_DK)[:, None], (RET_W, RET_W))
    head = jnp.arange(RET_W) // RET_DK
    bd = (head[:, None] == head[None, :]).astype(F32)
    gam4 = jnp.broadcast_to(jnp.exp(lg * 1.0)[None, :, None, None], (1, RET_HEADS, RET_DK, RET_DV))
    return dict(chunk=chunk, decay=decay, gq=gq, wk=wk, gl=gl, bd=bd, avg=bd / RET_DV, gam4=gam4)


def _block_diag(w):
    n, c, dd = w.shape
    out = jnp.zeros((n * c, n * dd), w.dtype)
    for i in range(n):
        out = out.at[i * c:(i + 1) * c, i * dd:(i + 1) * dd].set(w[i])
    return out


def _layer_weights(l, P):
    w_in = P["w_in"][l]
    bounds = [0, 256, 512, 768, 1024, 1280, 1408, 1440, 1696, 1952]
    kr_lo, kr_hi = bounds[6], bounds[7]
    w_in_p = jnp.concatenate(
        [w_in[:, :kr_lo], w_in[:, kr_hi:], w_in[:, kr_lo:kr_hi],
         jnp.zeros((w_in.shape[0], _N_IN_PAD - w_in.shape[1]), w_in.dtype)], axis=1)
    w_uq = P["mla_w_uq"][l]
    w_uk = P["mla_w_uk"][l]
    w_uv = P["mla_w_uv"][l]
    uk_pair = jnp.stack([_block_diag(jnp.stack([w_uk[:, 2 * p, :].T, w_uk[:, 2 * p + 1, :].T]))
                         for p in range(MLA_HEADS // 2)])
    uv_pair = jnp.stack([_block_diag(jnp.stack([w_uv[:, 2 * p, :], w_uv[:, 2 * p + 1, :]]))
                         for p in range(MLA_HEADS // 2)])
    row = lambda v: v.reshape(1, -1)
    return dict(
        norm1_w=row(P["norm1_w"][l]), w_in=w_in_p.astype(BF16),
        q_norm_w=row(P["mla_q_norm_w"][l]), kv_norm_w=row(P["mla_kv_norm_w"][l]),
        w_uq_nope=w_uq[:, :, :MLA_NOPE].reshape(MLA_Q_RANK, QN_W).astype(BF16),
        w_uq_rope=w_uq[:, :, MLA_NOPE:].reshape(MLA_Q_RANK, QR_W).astype(BF16),
        w_uk_pair=uk_pair.astype(BF16), w_uv_pair=uv_pair.astype(BF16),
        ret_gn_w=row(P["ret_gn_w"][l]),
        rg_conv_w=P["rg_conv_w"][l], rg_conv_b=row(P["rg_conv_b"][l]),
        rg_wa_bd=_block_diag(P["rg_w_a"][l]).astype(BF16), rg_b_a=row(P["rg_b_a"][l]),
        rg_wx_bd=_block_diag(P["rg_w_x"][l]).astype(BF16), rg_b_x=row(P["rg_b_x"][l]),
        rg_lambda=row(P["rg_lambda"][l]),
        w_out=P["w_out"][l].astype(BF16), norm2_w=row(P["norm2_w"][l]),
        ffn_w_up=P["ffn_w_up"][l].astype(BF16), ffn_conv_w=P["ffn_conv_w"][l],
        ffn_conv_b=row(P["ffn_conv_b"][l]), ffn_w_down=P["ffn_w_down"][l].astype(BF16),
    )


def _tile(n, pref):
    t = min(n, pref)
    assert n % t == 0, (n, pref)
    return t


def kernel(x_prompt, x_sample, cache_mla_latent, cache_mla_krope, page_table, state_ret, state_rg_conv,
           state_rglru, state_ffn_conv, norm1_w, w_in, ret_gn_w, mla_q_norm_w, mla_w_uq, mla_kv_norm_w,
           mla_w_uk, mla_w_uv, rg_conv_w, rg_conv_b, rg_w_a, rg_b_a, rg_w_x, rg_b_x, rg_lambda, w_out,
           norm2_w, ffn_w_up, ffn_conv_w, ffn_conv_b, ffn_w_down, final_norm_w):
    P = dict(norm1_w=norm1_w, w_in=w_in, ret_gn_w=ret_gn_w, mla_q_norm_w=mla_q_norm_w, mla_w_uq=mla_w_uq,
             mla_kv_norm_w=mla_kv_norm_w, mla_w_uk=mla_w_uk, mla_w_uv=mla_w_uv, rg_conv_w=rg_conv_w,
             rg_conv_b=rg_conv_b, rg_w_a=rg_w_a, rg_b_a=rg_b_a, rg_w_x=rg_w_x, rg_b_x=rg_b_x,
             rg_lambda=rg_lambda, w_out=w_out, norm2_w=norm2_w, ffn_w_up=ffn_w_up, ffn_conv_w=ffn_conv_w,
             ffn_conv_b=ffn_conv_b, ffn_w_down=ffn_w_down)
    depth = w_in.shape[0]
    B, S, D = x_prompt.shape
    DB, T, _ = x_sample.shape
    assert T == 1, "the sample path handles one new token per sequence"
    n_pages, page = page_table.shape[1], cache_mla_latent.shape[2]
    past_len = n_pages * page
    final_w = final_norm_w.reshape(1, D)
    weights = [_layer_weights(l, P) for l in range(depth)]

    chunk = RET_CHUNK if (S > RET_CHUNK and S % RET_CHUNK == 0) else S
    rt = _retention_tables(chunk)
    tabs_p = _rope_tables(jnp.arange(S, dtype=F32))
    tm_a, tt, tq, tm_e = _tile(S, 512), _tile(S, 512), _tile(S, 256), _tile(S, 512)
    ret_chunks_per_step = _tile(S // chunk, 4)
    kb = _tile(S, 512)
    x = x_prompt.reshape(B * S, D)
    p_new = []
    for l in range(depth):
        lw = weights[l]
        ret_in, qh, kcat, vt, p_lat, p_rope, rg_in = _inproj(x, B, tm_a, lw, tabs_p)
        y_ret, sbd = _retention_prompt(ret_in, B, lw, rt, ret_chunks_per_step)
        y_rg, p_rgc, p_rgh = _rglru_prompt(rg_in, B, tt, lw)
        y_mla = _attention_prompt(qh, kcat, vt, B, tq, kb, lw)
        x, p_ffc = _outproj_ffn(x, y_ret, y_mla, y_rg, lw, B, tm_e,
                                final_w if l == depth - 1 else None)
        s_ret = jnp.stack([sbd[:, h * RET_DK:(h + 1) * RET_DK, h * RET_DV:(h + 1) * RET_DV]
                           for h in range(RET_HEADS)], axis=1)
        p_new.append((p_lat.reshape(B, S, -1), p_rope.reshape(B, S, -1), s_ret, p_rgc,
                      p_rgh.reshape(B, -1), p_ffc))
    y_prompt = x.reshape(B, S, D)

    tabs_s = _rope_tables(jnp.full((DB,), past_len, dtype=F32))
    pages_per_chunk = _tile(n_pages, 64)
    krope_t = jnp.swapaxes(cache_mla_krope, 2, 3)
    x = x_sample.reshape(DB, D)
    s_new = []
    for l in range(depth):
        lw = weights[l]
        ret_in, qh, kcat, _, s_lat, s_rope, rg_in = _inproj(x, 1, DB, lw, tabs_s)
        y_ret, s_ret = _retention_step(ret_in, state_ret[l], lw, rt)
        y_rg, s_rgh = _rglru_step(rg_in, state_rg_conv[l], state_rglru[l], lw)
        o_lat = _attention_paged(qh, kcat, cache_mla_latent, krope_t, page_table, l,
                                 pages_per_chunk)
        y_mla = _uv_step(o_lat, lw)
        x, gate = _outproj_ffn(x, y_ret, y_mla, y_rg, lw, 1, DB,
                               final_w if l == depth - 1 else None, step_buf=state_ffn_conv[l])
        s_rgc = jnp.concatenate([state_rg_conv[l][:, 1:], rg_in[:, None, 0:RG_WIDTH]], axis=1)
        s_ffc = jnp.concatenate([state_ffn_conv[l][:, 1:], gate[:, None, :]], axis=1)
        s_new.append((s_lat.reshape(DB, T, -1), s_rope.reshape(DB, T, -1), s_ret, s_rgc, s_rgh, s_ffc))
    y_sample = x.reshape(DB, T, D)

    stack = lambda items, i: jnp.stack([it[i] for it in items])
    return (y_prompt, y_sample, *[stack(p_new, i) for i in range(6)], *[stack(s_new, i) for i in range(6)])
```

```python
import functools

import jax
import jax.numpy as jnp
from jax import lax
from jax.experimental import pallas as pl
from jax.experimental.pallas import tpu as pltpu

RET_HEADS = 4
RET_DK = 64
RET_DV = 64
RET_CHUNK = 128
MLA_HEADS = 8
MLA_NOPE = 64
MLA_ROPE = 32
MLA_V = 64
MLA_Q_RANK = 256
MLA_KV_RANK = 128
MLA_SCALE = (MLA_NOPE + MLA_ROPE) ** -0.5
LOG2_E = 1.4426950408889634
Q_SCALE = MLA_SCALE * LOG2_E
RG_WIDTH = 256
RG_BLOCKS = 4
RG_CONV = 4
RG_C = 8.0
FFN_CONV = 3
ROPE_BASE = 10000.0
EPS = 1e-6
NEG_INF = -1e30

RET_W = RET_HEADS * RET_DK
QR_W = MLA_HEADS * MLA_ROPE
QN_W = MLA_HEADS * MLA_NOPE
KCAT_W = 256
LANE = 128
VMEM_LIMIT = 56 * 1024 * 1024
PAGED_SLOTS = 3

BF16 = jnp.bfloat16
F32 = jnp.float32

_OFF_RQ, _OFF_RK, _OFF_RV, _OFF_RG = 0, 256, 512, 768
_OFF_CQ, _OFF_CKV, _OFF_GX, _OFF_GG, _OFF_KR = 1024, 1280, 1408, 1664, 1920
_N_IN_PAD = 2048


def _params(n_axes):
    return pltpu.CompilerParams(dimension_semantics=("arbitrary",) * n_axes,
                                vmem_limit_bytes=VMEM_LIMIT)


def _const_spec(shape):
    n = len(shape)
    return pl.BlockSpec(shape, lambda *_: (0,) * n, pipeline_mode=pl.Buffered(1))


def _dot(a, b):
    return jnp.dot(a, b, preferred_element_type=F32)


def _dot_nt(a, b):
    return lax.dot_general(a, b, (((1,), (1,)), ((), ())), preferred_element_type=F32)


def _dot_tn(a, b):
    return lax.dot_general(a, b, (((0,), (0,)), ((), ())), preferred_element_type=F32)


def _rms(x, w):
    return x * lax.rsqrt(jnp.mean(x * x, axis=-1, keepdims=True) + EPS) * w


def _rope3(x, tab_ref, half):
    w = x.shape[-1]
    return (x * tab_ref[0] + pltpu.roll(x, w - half, 1) * tab_ref[1]
            + pltpu.roll(x, half, 1) * tab_ref[2])


def _inproj_kernel(x_ref, n1_ref, win_ref, tabr_ref, tabq_ref, tabk_ref, qnw_ref, wuqn_ref,
                   wuqr_ref, wukp_ref, kvnw_ref,
                   ret_ref, q_ref, kcat_ref, vt_ref, plat_ref, prope_ref, rg_ref):
    h = _rms(x_ref[...], n1_ref[...]).astype(BF16)
    z = _dot(h, win_ref[...])

    ret_ref[:, _OFF_RQ:_OFF_RQ + RET_W] = _rope3(z[:, _OFF_RQ:_OFF_RQ + RET_W], tabr_ref, RET_DK // 2)
    ret_ref[:, _OFF_RK:_OFF_RK + RET_W] = (
        _rope3(z[:, _OFF_RK:_OFF_RK + RET_W], tabr_ref, RET_DK // 2) * (RET_DK ** -0.5))
    ret_ref[:, _OFF_RV:_OFF_CQ] = z[:, _OFF_RV:_OFF_CQ]
    rg_ref[...] = z[:, _OFF_GX:_OFF_KR]

    ckvn = _rms(z[:, _OFF_CKV:_OFF_CKV + MLA_KV_RANK], kvnw_ref[...])
    plat_ref[...] = ckvn
    krr = _rope3(z[:, _OFF_KR:_OFF_KR + LANE], tabk_ref, MLA_ROPE // 2)
    prope_ref[...] = krr[:, :MLA_ROPE]
    kcat_ref[:, 0:MLA_KV_RANK] = ckvn.astype(BF16)
    kcat_ref[:, MLA_KV_RANK:KCAT_W] = krr.astype(BF16)
    vt_ref[...] = ckvn.T.astype(BF16)

    cqn = _rms(z[:, _OFF_CQ:_OFF_CQ + MLA_Q_RANK], qnw_ref[...]).astype(BF16)
    qn = _dot(cqn, wuqn_ref[...]).astype(BF16)
    qr = _rope3(_dot(cqn, wuqr_ref[...]), tabq_ref, MLA_ROPE // 2) * Q_SCALE
    lane = lax.broadcasted_iota(jnp.int32, (1, LANE), 1)
    heads_per_tile = LANE // MLA_ROPE
    for p in range(MLA_HEADS // 2):
        ql = _dot(qn[:, p * LANE:(p + 1) * LANE], wukp_ref[p]) * Q_SCALE
        for e in range(2):
            hd = 2 * p + e
            q_ref[hd, :, 0:MLA_KV_RANK] = ql[:, e * LANE:(e + 1) * LANE].astype(BF16)
            tile = qr[:, (hd // heads_per_tile) * LANE:(hd // heads_per_tile + 1) * LANE]
            sh = MLA_ROPE * (hd % heads_per_tile)
            if sh:
                tile = pltpu.roll(tile, LANE - sh, 1)
            q_ref[hd, :, MLA_KV_RANK:KCAT_W] = jnp.where(lane < MLA_ROPE, tile, 0.0).astype(BF16)


def _inproj(x2d, n_batch, tm, lw, tabs):
    rows, d = x2d.shape
    ns = rows // (n_batch * tm)
    tabr, tabq, tabk = tabs
    row_map = lambda s, b: (b * ns + s, 0)
    tab_map = lambda s, b: (0, s, 0)
    out_shape = (
        jax.ShapeDtypeStruct((rows, 4 * RET_W), F32),
        jax.ShapeDtypeStruct((MLA_HEADS, rows, KCAT_W), BF16),
        jax.ShapeDtypeStruct((rows, KCAT_W), BF16),
        jax.ShapeDtypeStruct((MLA_KV_RANK, rows), BF16),
        jax.ShapeDtypeStruct((rows, MLA_KV_RANK), F32),
        jax.ShapeDtypeStruct((rows, MLA_ROPE), F32),
        jax.ShapeDtypeStruct((rows, 2 * RG_WIDTH), F32),
    )
    return pl.pallas_call(
        _inproj_kernel,
        grid=(ns, n_batch),
        in_specs=[
            pl.BlockSpec((tm, d), row_map),
            _const_spec((1, d)),
            _const_spec((d, _N_IN_PAD)),
            pl.BlockSpec((3, tm, RET_W), tab_map),
            pl.BlockSpec((3, tm, QR_W), tab_map),
            pl.BlockSpec((3, tm, LANE), tab_map),
            _const_spec((1, MLA_Q_RANK)),
            _const_spec((MLA_Q_RANK, QN_W)),
            _const_spec((MLA_Q_RANK, QR_W)),
            _const_spec((MLA_HEADS // 2, LANE, 2 * MLA_KV_RANK)),
            _const_spec((1, MLA_KV_RANK)),
        ],
        out_specs=[
            pl.BlockSpec((tm, 4 * RET_W), row_map),
            pl.BlockSpec((MLA_HEADS, tm, KCAT_W), lambda s, b: (0, b * ns + s, 0)),
            pl.BlockSpec((tm, KCAT_W), row_map),
            pl.BlockSpec((MLA_KV_RANK, tm), lambda s, b: (0, b * ns + s)),
            pl.BlockSpec((tm, MLA_KV_RANK), row_map),
            pl.BlockSpec((tm, MLA_ROPE), row_map),
            pl.BlockSpec((tm, 2 * RG_WIDTH), row_map),
        ],
        out_shape=out_shape,
        compiler_params=_params(2),
        name="inproj",
    )(x2d, lw["norm1_w"], lw["w_in"], tabr, tabq, tabk, lw["q_norm_w"], lw["w_uq_nope"],
      lw["w_uq_rope"], lw["w_uk_pair"], lw["kv_norm_w"])


def _group_norm_gate(o, g, gnw, avg):
    mu = _dot(o, avg)
    d = o - mu
    var = _dot(d * d, avg)
    return d * lax.rsqrt(var + EPS) * gnw * (g * jax.nn.sigmoid(g))


def _ret_kernel(ret_ref, dec_ref, gq_ref, wk_ref, gl_ref, bd_ref, gnw_ref, avg_ref,
                y_ref, sbd_ref, s_scr):
    @pl.when(pl.program_id(1) == 0)
    def _():
        s_scr[...] = jnp.zeros_like(s_scr)

    L = dec_ref.shape[1]
    head_of_lane = lax.broadcasted_iota(jnp.int32, (1, RET_W), 1) // RET_DK
    s_old = s_scr[...]
    for cc in range(ret_ref.shape[0] // L):
        rows = pl.ds(cc * L, L)
        q = ret_ref[rows, _OFF_RQ:_OFF_RQ + RET_W]
        k = ret_ref[rows, _OFF_RK:_OFF_RK + RET_W]
        vb = ret_ref[rows, _OFF_RV:_OFF_RV + RET_W].astype(BF16)
        g = ret_ref[rows, _OFF_RG:_OFF_RG + RET_W]
        kb = k.astype(BF16)
        o = _dot(q.astype(BF16), s_old.astype(BF16)) * gq_ref[...]
        for h in range(RET_HEADS):
            mh = head_of_lane == h
            s = _dot_nt(jnp.where(mh, q, 0.0).astype(BF16), kb) * dec_ref[h]
            o = o + jnp.where(mh, _dot(s.astype(BF16), vb), 0.0)
        upd = _dot_tn((k * wk_ref[...]).astype(BF16), vb)
        s_old = gl_ref[...] * s_old + bd_ref[...] * upd
        y_ref[rows, :] = _group_norm_gate(o, g, gnw_ref[...], avg_ref[...])
    s_scr[...] = s_old
    sbd_ref[...] = s_old


def _retention_prompt(ret_in, n_batch, lw, rt, chunks_per_step):
    rows = ret_in.shape[0]
    L = rt["chunk"]
    tb = L * chunks_per_step
    nc = rows // (n_batch * tb)
    return pl.pallas_call(
        _ret_kernel,
        grid=(n_batch, nc),
        in_specs=[
            pl.BlockSpec((tb, 4 * RET_W), lambda b, c: (b * nc + c, 0)),
            _const_spec((RET_HEADS, L, L)),
            _const_spec((L, RET_W)),
            _const_spec((L, RET_W)),
            _const_spec((RET_W, RET_W)),
            _const_spec((RET_W, RET_W)),
            _const_spec((1, RET_W)),
            _const_spec((RET_W, RET_W)),
        ],
        out_specs=[
            pl.BlockSpec((tb, RET_W), lambda b, c: (b * nc + c, 0)),
            pl.BlockSpec((None, RET_W, RET_W), lambda b, c: (b, 0, 0)),
        ],
        out_shape=(jax.ShapeDtypeStruct((rows, RET_W), F32),
                   jax.ShapeDtypeStruct((n_batch, RET_W, RET_W), F32)),
        scratch_shapes=[pltpu.VMEM((RET_W, RET_W), F32)],
        compiler_params=_params(2),
        name="retention",
    )(ret_in, rt["decay"], rt["gq"], rt["wk"], rt["gl"], rt["bd"], lw["ret_gn_w"], rt["avg"])


def _ret_step_kernel(q_ref, k_ref, v_ref, g_ref, s0_ref, gam_ref, gnw_ref, y_ref, s_ref):
    gam = gam_ref[...]
    v = v_ref[...]
    o = jnp.zeros_like(v)
    for d in range(RET_DK):
        s_d = gam * s0_ref[d] + k_ref[d:d + 1, :] * v
        s_ref[d] = s_d
        o = o + q_ref[d:d + 1, :] * s_d
    mu = jnp.mean(o, axis=0, keepdims=True)
    dev = o - mu
    var = jnp.mean(dev * dev, axis=0, keepdims=True)
    g = g_ref[...]
    y_ref[...] = dev * lax.rsqrt(var + EPS) * gnw_ref[...] * (g * jax.nn.sigmoid(g))


def _retention_step(ret_in, s0_t, layer, lw, rt):
    n = ret_in.shape[0]
    heads_t = lambda off: ret_in[:, off:off + RET_W].T.reshape(RET_HEADS, RET_DK, n)
    gnw_t = jnp.broadcast_to(lw["ret_gn_w"].reshape(RET_HEADS, RET_DV, 1), (RET_HEADS, RET_DV, n))
    gam_t = jnp.broadcast_to(rt["gam"][:, None, None], (RET_HEADS, 1, n))
    vec = pl.BlockSpec((None, RET_DK, n), lambda h: (h, 0, 0))
    mat = pl.BlockSpec((None, RET_DK, RET_DV, n), lambda h: (h, 0, 0, 0))
    y_t, s_t = pl.pallas_call(
        _ret_step_kernel,
        grid=(RET_HEADS,),
        in_specs=[vec, vec, vec, vec,
                  pl.BlockSpec((None, None, RET_DK, RET_DV, n), lambda h: (layer, h, 0, 0, 0)),
                  pl.BlockSpec((None, 1, n), lambda h: (h, 0, 0)), vec],
        out_specs=[vec, mat],
        out_shape=(jax.ShapeDtypeStruct((RET_HEADS, RET_DV, n), F32),
                   jax.ShapeDtypeStruct((RET_HEADS, RET_DK, RET_DV, n), F32)),
        compiler_params=_params(1),
        name="retention_step",
    )(heads_t(_OFF_RQ), heads_t(_OFF_RK), heads_t(_OFF_RV), heads_t(_OFF_RG), s0_t, gam_t, gnw_t)
    return y_t.reshape(RET_W, n).T, jnp.transpose(s_t, (3, 0, 1, 2))


def _rg_gates(xc, wa_ref, ba_ref, wx_ref, bx_ref, lam_ref):
    xb = xc.astype(BF16)
    r = jax.nn.sigmoid(_dot(xb, wa_ref[...]) + ba_ref[...])
    i = jax.nn.sigmoid(_dot(xb, wx_ref[...]) + bx_ref[...])
    y = -lam_ref[...]
    softplus = jnp.maximum(y, 0.0) + jnp.log1p(jnp.exp(-jnp.abs(y)))
    log_a = -RG_C * r * softplus
    a = jnp.exp(log_a)
    u = jnp.sqrt(1.0 - jnp.exp(2.0 * log_a)) * i * xc
    return a, u


def _rg_kernel(rg_ref, cw_ref, cb_ref, wa_ref, ba_ref, wx_ref, bx_ref, lam_ref,
               y_ref, buf_ref, hlast_ref, xs_scr, h_scr):
    tt = rg_ref.shape[0]
    halo = 8

    @pl.when(pl.program_id(1) == 0)
    def _():
        xs_scr[0:halo, :] = jnp.zeros((halo, RG_WIDTH), F32)
        h_scr[...] = jnp.zeros_like(h_scr)

    gx = rg_ref[:, 0:RG_WIDTH]
    gg = rg_ref[:, RG_WIDTH:2 * RG_WIDTH]
    xs_scr[halo:halo + tt, :] = gx
    xc = cb_ref[...] + gx * cw_ref[RG_CONV - 1:RG_CONV, :]
    for j in range(1, RG_CONV):
        xc = xc + xs_scr[halo - j:halo - j + tt, :] * cw_ref[RG_CONV - 1 - j:RG_CONV - j, :]
    xs_scr[0:halo, :] = xs_scr[tt:tt + halo, :]
    buf_ref[...] = gx[tt - (RG_CONV - 1):tt, :]

    a, u = _rg_gates(xc, wa_ref, ba_ref, wx_ref, bx_ref, lam_ref)
    row = lax.broadcasted_iota(jnp.int32, (tt, 1), 0)
    step = 1
    while step < tt:
        keep = row >= step
        a_sh = jnp.where(keep, pltpu.roll(a, step, 0), 1.0)
        u_sh = jnp.where(keep, pltpu.roll(u, step, 0), 0.0)
        u = a * u_sh + u
        a = a * a_sh
        step *= 2
    h = u + a * h_scr[...]
    h_scr[...] = h[tt - 1:tt, :]
    hlast_ref[...] = h[tt - 1:tt, :]
    y_ref[...] = h * jax.nn.gelu(gg, approximate=True)


def _rglru_prompt(rg_in, n_batch, tt, lw):
    rows = rg_in.shape[0]
    nt = rows // (n_batch * tt)
    w = RG_WIDTH
    return pl.pallas_call(
        _rg_kernel,
        grid=(n_batch, nt),
        in_specs=[
            pl.BlockSpec((tt, 2 * w), lambda b, t: (b * nt + t, 0)),
            _const_spec((RG_CONV, w)), _const_spec((1, w)),
            _const_spec((w, w)), _const_spec((1, w)),
            _const_spec((w, w)), _const_spec((1, w)), _const_spec((1, w)),
        ],
        out_specs=[
            pl.BlockSpec((tt, w), lambda b, t: (b * nt + t, 0)),
            pl.BlockSpec((None, RG_CONV - 1, w), lambda b, t: (b, 0, 0)),
            pl.BlockSpec((None, 1, w), lambda b, t: (b, 0, 0)),
        ],
        out_shape=(jax.ShapeDtypeStruct((rows, w), F32),
                   jax.ShapeDtypeStruct((n_batch, RG_CONV - 1, w), F32),
                   jax.ShapeDtypeStruct((n_batch, 1, w), F32)),
        scratch_shapes=[pltpu.VMEM((tt + 8, w), F32), pltpu.VMEM((1, w), F32)],
        compiler_params=_params(2),
        name="rglru",
    )(rg_in, lw["rg_conv_w"], lw["rg_conv_b"], lw["rg_wa_bd"], lw["rg_b_a"], lw["rg_wx_bd"],
      lw["rg_b_x"], lw["rg_lambda"])


def _rg_step_kernel(rg_ref, b0_ref, b1_ref, b2_ref, h0_ref, cw_ref, cb_ref, wa_ref, ba_ref,
                    wx_ref, bx_ref, lam_ref, y_ref, h_ref):
    gx = rg_ref[:, 0:RG_WIDTH]
    gg = rg_ref[:, RG_WIDTH:2 * RG_WIDTH]
    xc = (cb_ref[...] + b0_ref[...] * cw_ref[0:1, :] + b1_ref[...] * cw_ref[1:2, :]
          + b2_ref[...] * cw_ref[2:3, :] + gx * cw_ref[3:4, :])
    a, u = _rg_gates(xc, wa_ref, ba_ref, wx_ref, bx_ref, lam_ref)
    h = u + a * h0_ref[...]
    h_ref[...] = h
    y_ref[...] = h * jax.nn.gelu(gg, approximate=True)


def _rglru_step(rg_in, buf0, h0, lw):
    n = rg_in.shape[0]
    w = RG_WIDTH
    full = lambda shape: pl.BlockSpec(shape, lambda i: (0,) * len(shape))
    return pl.pallas_call(
        _rg_step_kernel,
        grid=(1,),
        in_specs=[full((n, 2 * w)), full((n, w)), full((n, w)), full((n, w)), full((n, w)),
                  full((RG_CONV, w)), full((1, w)), full((w, w)), full((1, w)), full((w, w)),
                  full((1, w)), full((1, w))],
        out_specs=[full((n, w)), full((n, w))],
        out_shape=(jax.ShapeDtypeStruct((n, w), F32), jax.ShapeDtypeStruct((n, w), F32)),
        compiler_params=_params(1),
        name="rglru_step",
    )(rg_in, buf0[:, 0], buf0[:, 1], buf0[:, 2], h0, lw["rg_conv_w"], lw["rg_conv_b"],
      lw["rg_wa_bd"], lw["rg_b_a"], lw["rg_wx_bd"], lw["rg_b_x"], lw["rg_lambda"])


def _uv_project_t(o_t, tq, wuv_ref, out_ref):
    for p in range(MLA_HEADS // 2):
        pair_t = jnp.concatenate([o_t[:, (2 * p) * tq:(2 * p + 1) * tq],
                                  o_t[:, (2 * p + 1) * tq:(2 * p + 2) * tq]], axis=0).astype(BF16)
        out_ref[:, p * LANE:(p + 1) * LANE] = _dot_tn(pair_t, wuv_ref[p])


def _attn_kernel(q_ref, k_ref, vt_ref, wuv_ref, o_ref, sa_scr, sb_scr, m_scr, l_scr, acc_scr, *, kb):
    tq = q_ref.shape[1]
    assert tq & (tq - 1) == 0 and kb % tq == 0
    m_cols = MLA_HEADS * tq
    i = pl.program_id(1)
    q = q_ref[...].reshape(m_cols, KCAT_W)
    m_scr[...] = jnp.full_like(m_scr, NEG_INF)
    l_scr[...] = jnp.zeros_like(l_scr)
    acc_scr[...] = jnp.zeros_like(acc_scr)

    def scores(dst, j):
        dst[...] = _dot_nt(k_ref[pl.ds(pl.multiple_of(j * kb, kb), kb), :], q)

    def absorb(src, j, masked):
        s = src[...]
        if masked:
            krow = lax.broadcasted_iota(jnp.int32, (kb, m_cols), 0) + (j * kb - i * tq)
            qcol = lax.broadcasted_iota(jnp.int32, (kb, m_cols), 1) & (tq - 1)
            s = jnp.where(krow <= qcol, s, NEG_INF)
        m_prev = m_scr[...]
        m_next = jnp.maximum(m_prev, jnp.max(s, axis=0, keepdims=True))
        p = jnp.exp2(s - m_next)
        alpha = jnp.exp2(m_prev - m_next)
        l_scr[...] = alpha * l_scr[...] + jnp.sum(p, axis=0, keepdims=True)
        vt = vt_ref[:, pl.ds(pl.multiple_of(j * kb, kb), kb)]
        acc_scr[...] = alpha * acc_scr[...] + _dot(vt, p.astype(BF16))
        m_scr[...] = m_next

    n_blocks = (i * tq) // kb + 1
    n_pairs = (n_blocks - 1) // 2
    scores(sa_scr, 0)

    def pair(t, carry):
        scores(sb_scr, 2 * t + 1)
        absorb(sa_scr, 2 * t, False)
        scores(sa_scr, 2 * t + 2)
        absorb(sb_scr, 2 * t + 1, False)
        return carry

    lax.fori_loop(0, n_pairs, pair, 0)
    odd_left = (n_blocks - 1) - 2 * n_pairs

    @pl.when(odd_left == 0)
    def _():
        absorb(sa_scr, n_blocks - 1, True)

    @pl.when(odd_left == 1)
    def _():
        scores(sb_scr, n_blocks - 1)
        absorb(sa_scr, n_blocks - 2, False)
        absorb(sb_scr, n_blocks - 1, True)

    o_t = acc_scr[...] / l_scr[...]
    _uv_project_t(o_t, tq, wuv_ref, o_ref)


def _attention_prompt(qh, kcat, vt, n_batch, tq, kb, lw):
    rows = kcat.shape[0]
    s_len = rows // n_batch
    nq = s_len // tq
    m_cols = MLA_HEADS * tq
    return pl.pallas_call(
        functools.partial(_attn_kernel, kb=kb),
        grid=(n_batch, nq),
        in_specs=[
            pl.BlockSpec((MLA_HEADS, tq, KCAT_W), lambda b, i: (0, b * nq + i, 0)),
            pl.BlockSpec((s_len, KCAT_W), lambda b, i: (b, 0)),
            pl.BlockSpec((MLA_KV_RANK, s_len), lambda b, i: (0, b)),
            _const_spec((MLA_HEADS // 2, 2 * MLA_KV_RANK, LANE)),
        ],
        out_specs=pl.BlockSpec((tq, MLA_HEADS * MLA_V), lambda b, i: (b * nq + i, 0)),
        out_shape=jax.ShapeDtypeStruct((rows, MLA_HEADS * MLA_V), F32),
        scratch_shapes=[pltpu.VMEM((kb, m_cols), F32), pltpu.VMEM((kb, m_cols), F32),
                        pltpu.VMEM((1, m_cols), F32), pltpu.VMEM((1, m_cols), F32),
                        pltpu.VMEM((MLA_KV_RANK, m_cols), F32)],
        compiler_params=_params(2),
        name="attention",
    )(qh, kcat, vt, lw["w_uv_pair"])


def _paged_kernel(pt_ref, q_ref, kself_ref, lat_hbm, krt_hbm, o_ref,
                  lat_buf, krt_buf, sem, m_scr, l_scr, acc_scr, *, layer, pages_per_chunk,
                  chunks_per_seq, sub_keys):
    g = pl.program_id(0)
    page = lat_hbm.shape[2]
    chunk_keys = pages_per_chunk * page
    n_slots = lat_buf.shape[0]
    ahead = n_slots - 1

    last = pl.num_programs(0) - 1
    slot = g % n_slots
    fill = (g + ahead) % n_slots
    n_sub = chunk_keys // sub_keys
    pages_per_sub = pages_per_chunk // n_sub

    def page_copies(pid, dst_slot, j):
        cols = pl.ds(j * page, page)
        return (pltpu.make_async_copy(lat_hbm.at[layer, pid], lat_buf.at[dst_slot, cols, :], sem.at[dst_slot, 0]),
                pltpu.make_async_copy(krt_hbm.at[layer, pid], krt_buf.at[dst_slot, :, cols], sem.at[dst_slot, 1]))

    def start_pages(step, dst_slot, pages):
        for j in pages:
            for cp in page_copies(pt_ref[step * pages_per_chunk + j], dst_slot, j):
                cp.start()

    def wait_chunk(dst_slot):
        for j in range(pages_per_chunk):
            for cp in page_copies(0, dst_slot, j):
                cp.wait()

    @pl.when(g == 0)
    def _():
        for a in range(ahead):
            start_pages(jnp.minimum(a, last), a, range(pages_per_chunk))

    wait_chunk(slot)
    c = g % chunks_per_seq

    @pl.when(c == 0)
    def _():
        m_scr[...] = jnp.full_like(m_scr, NEG_INF)
        l_scr[...] = jnp.zeros_like(l_scr)
        acc_scr[...] = jnp.zeros_like(acc_scr)

    q = q_ref[...]
    q_lat = q[:, 0:MLA_KV_RANK]
    q_rope = q[:, MLA_KV_RANK:MLA_KV_RANK + MLA_ROPE]
    nxt = jnp.minimum(g + ahead, last)
    lats, scores = [], []
    for sb in range(n_sub):
        start_pages(nxt, fill, range(sb * pages_per_sub, (sb + 1) * pages_per_sub))
        keys = pl.ds(sb * sub_keys, sub_keys)
        lat = lat_buf[slot, keys, :].astype(BF16)
        krt = krt_buf[slot, :, keys].astype(BF16)
        lats.append(lat)
        scores.append(_dot_nt(q_lat, lat) + _dot(q_rope, krt))
    m_prev = m_scr[...]
    m_cur = functools.reduce(jnp.maximum, [jnp.max(s, axis=1, keepdims=True) for s in scores])
    m_next = jnp.maximum(m_prev, m_cur)
    alpha = jnp.exp2(m_prev - m_next)
    l_new = alpha * l_scr[...]
    acc = alpha * acc_scr[...]
    for s, lat in zip(scores, lats):
        p = jnp.exp2(s - m_next)
        l_new = l_new + jnp.sum(p, axis=1, keepdims=True)
        acc = acc + _dot(p.astype(BF16), lat)
    m_scr[...] = m_next
    l_scr[...] = l_new
    acc_scr[...] = acc

    @pl.when(g == last)
    def _():
        for a in range(1, n_slots):
            wait_chunk((g + a) % n_slots)

    @pl.when(c == chunks_per_seq - 1)
    def _():
        kself = kself_ref[...].astype(F32)
        s_self = jnp.sum(q.astype(F32) * kself, axis=1, keepdims=True)
        m_prev = m_scr[...]
        m_fin = jnp.maximum(m_prev, s_self)
        alpha = jnp.exp2(m_prev - m_fin)
        p_self = jnp.exp2(s_self - m_fin)
        l_fin = alpha * l_scr[...] + p_self
        acc_fin = alpha * acc_scr[...] + p_self * kself[:, 0:MLA_KV_RANK]
        o_ref[...] = acc_fin / l_fin


def _attention_paged(qh, kcat, pool_lat, pool_rope_t, page_table, layer, pages_per_chunk):
    n = kcat.shape[0]
    n_pages = page_table.shape[1]
    page, rank = pool_lat.shape[2], pool_lat.shape[3]
    rope = pool_rope_t.shape[2]
    ppc = pages_per_chunk
    cps = n_pages // ppc
    chunk_keys = ppc * page
    q3 = jnp.transpose(qh, (1, 0, 2))
    kself = kcat.reshape(n, 1, KCAT_W)
    pt_flat = page_table.reshape(-1)
    seq_map = lambda g, pt: (g // cps, 0, 0)
    grid_spec = pltpu.PrefetchScalarGridSpec(
        num_scalar_prefetch=1,
        grid=(n * cps,),
        in_specs=[pl.BlockSpec((None, MLA_HEADS, KCAT_W), seq_map),
                  pl.BlockSpec((None, 1, KCAT_W), seq_map),
                  pl.BlockSpec(memory_space=pl.ANY),
                  pl.BlockSpec(memory_space=pl.ANY)],
        out_specs=pl.BlockSpec((None, MLA_HEADS, MLA_KV_RANK), seq_map),
        scratch_shapes=[pltpu.VMEM((PAGED_SLOTS, chunk_keys, rank), F32),
                        pltpu.VMEM((PAGED_SLOTS, rope, chunk_keys), F32),
                        pltpu.SemaphoreType.DMA((PAGED_SLOTS, 2)),
                        pltpu.VMEM((MLA_HEADS, 1), F32), pltpu.VMEM((MLA_HEADS, 1), F32),
                        pltpu.VMEM((MLA_HEADS, MLA_KV_RANK), F32)],
    )
    return pl.pallas_call(
        functools.partial(_paged_kernel, layer=layer, pages_per_chunk=ppc, chunks_per_seq=cps,
                          sub_keys=min(chunk_keys, 2048)),
        grid_spec=grid_spec,
        out_shape=jax.ShapeDtypeStruct((n, MLA_HEADS, MLA_KV_RANK), F32),
        compiler_params=_params(1),
        name="attention_paged",
    )(pt_flat, q3, kself, pool_lat, pool_rope_t)


def _uv_kernel(o_ref, wuv_ref, y_ref):
    for p in range(MLA_HEADS // 2):
        pair = o_ref[:, 2 * p * MLA_KV_RANK:(2 * p + 2) * MLA_KV_RANK].astype(BF16)
        y_ref[:, p * LANE:(p + 1) * LANE] = _dot(pair, wuv_ref[p])


def _uv_step(o_lat, lw):
    n = o_lat.shape[0]
    o2 = o_lat.reshape(n, MLA_HEADS * MLA_KV_RANK)
    full = lambda shape: pl.BlockSpec(shape, lambda i: (0,) * len(shape))
    return pl.pallas_call(
        _uv_kernel,
        grid=(1,),
        in_specs=[full(o2.shape), full((MLA_HEADS // 2, 2 * MLA_KV_RANK, LANE))],
        out_specs=full((n, MLA_HEADS * MLA_V)),
        out_shape=jax.ShapeDtypeStruct((n, MLA_HEADS * MLA_V), F32),
        compiler_params=_params(1),
        name="uv_step",
    )(o2, lw["w_uv_pair"])


def _ff_bounds(d_ff, max_tiles=4, tile=256):
    if d_ff % tile:
        return (0, d_ff)
    tiles = d_ff // tile
    n = -(-tiles // max_tiles)
    sizes = [tiles // n + (1 if c >= n - tiles % n else 0) for c in range(n)]
    bounds = [0]
    for s in sizes:
        bounds.append(bounds[-1] + s * tile)
    return tuple(bounds)


def _ffn_kernel(*refs, stepwise, final_norm, bounds):
    x_ref, yret_ref, ymla_ref, yrg_ref, wout_ref, n2_ref, wup_ref, cw_ref, cb_ref, wdn_ref = refs[:10]
    pos = 10
    fn_ref = None
    if final_norm:
        fn_ref = refs[pos]
        pos += 1
    if stepwise:
        bm2_ref, bm1_ref = refs[pos:pos + 2]
        out_ref, gate_ref = refs[pos + 2:pos + 4]
    else:
        out_ref, tail_ref, g_scr, carry_scr = refs[pos:pos + 4]
    tm = x_ref.shape[0]
    d_ff = wdn_ref.shape[0]
    halo = 8

    if not stepwise:
        @pl.when(pl.program_id(1) == 0)
        def _():
            carry_scr[...] = jnp.zeros_like(carry_scr)

    mix = jnp.concatenate([yret_ref[...], ymla_ref[...], yrg_ref[...]], axis=1).astype(BF16)
    x1 = x_ref[...] + _dot(mix, wout_ref[...])
    h2 = _rms(x1, n2_ref[...]).astype(BF16)
    acc = jnp.zeros_like(x1)
    for c in range(len(bounds) - 1):
        lo, hi = bounds[c], bounds[c + 1]
        tf = hi - lo
        gate = _dot(h2, wup_ref[:, lo:hi])
        val = _dot(h2, wup_ref[:, d_ff + lo:d_ff + hi])
        if stepwise:
            gm2 = bm2_ref[:, lo:hi]
            gm1 = bm1_ref[:, lo:hi]
            gate_ref[:, lo:hi] = gate
        else:
            g_scr[0:halo, 0:tf] = carry_scr[c, :, 0:tf]
            g_scr[halo:halo + tm, 0:tf] = gate
            gm1 = g_scr[halo - 1:halo - 1 + tm, 0:tf]
            gm2 = g_scr[halo - 2:halo - 2 + tm, 0:tf]
            carry_scr[c, :, 0:tf] = g_scr[tm:tm + halo, 0:tf]
            tail_ref[:, lo:hi] = gate[tm - (FFN_CONV - 1):tm, :]
        gc = (cb_ref[:, lo:hi] + gm2 * cw_ref[0:1, lo:hi] + gm1 * cw_ref[1:2, lo:hi]
              + gate * cw_ref[2:3, lo:hi])
        act = (gc * jax.nn.sigmoid(gc) * val).astype(BF16)
        acc = acc + _dot(act, wdn_ref[lo:hi, :])
    x2 = x1 + acc
    out_ref[...] = _rms(x2, fn_ref[...]) if final_norm else x2


def _outproj_ffn(x2d, y_ret, y_mla, y_rg, lw, n_batch, tm, final_w, step_buf=None):
    rows, d = x2d.shape
    d_ff = lw["ffn_w_down"].shape[0]
    bounds = _ff_bounds(d_ff)
    n_chunks = len(bounds) - 1
    tf = max(bounds[c + 1] - bounds[c] for c in range(n_chunks))
    stepwise = step_buf is not None
    final_norm = final_w is not None
    nt = rows // (n_batch * tm)
    row_map = lambda b, t: (b * nt + t, 0)
    in_specs = [
        pl.BlockSpec((tm, d), row_map),
        pl.BlockSpec((tm, RET_W), row_map),
        pl.BlockSpec((tm, MLA_HEADS * MLA_V), row_map),
        pl.BlockSpec((tm, RG_WIDTH), row_map),
        _const_spec((d, d)), _const_spec((1, d)), _const_spec((d, 2 * d_ff)),
        _const_spec((FFN_CONV, d_ff)), _const_spec((1, d_ff)), _const_spec((d_ff, d)),
    ]
    args = [x2d, y_ret, y_mla, y_rg, lw["w_out"], lw["norm2_w"], lw["ffn_w_up"], lw["ffn_conv_w"],
            lw["ffn_conv_b"], lw["ffn_w_down"]]
    if final_norm:
        in_specs.append(_const_spec((1, d)))
        args.append(final_w)
    if stepwise:
        in_specs += [pl.BlockSpec((tm, d_ff), row_map)] * 2
        args += [step_buf[:, 0], step_buf[:, 1]]
        out_specs = [pl.BlockSpec((tm, d), row_map), pl.BlockSpec((tm, d_ff), row_map)]
        out_shape = (jax.ShapeDtypeStruct((rows, d), F32), jax.ShapeDtypeStruct((rows, d_ff), F32))
        scratch = []
    else:
        out_specs = [pl.BlockSpec((tm, d), row_map),
                     pl.BlockSpec((None, FFN_CONV - 1, d_ff), lambda b, t: (b, 0, 0))]
        out_shape = (jax.ShapeDtypeStruct((rows, d), F32),
                     jax.ShapeDtypeStruct((n_batch, FFN_CONV - 1, d_ff), F32))
        scratch = [pltpu.VMEM((tm + 8, tf), F32), pltpu.VMEM((n_chunks, 8, tf), F32)]
    return pl.pallas_call(
        functools.partial(_ffn_kernel, stepwise=stepwise, final_norm=final_norm, bounds=bounds),
        grid=(n_batch, nt),
        in_specs=in_specs,
        out_specs=out_specs,
        out_shape=out_shape,
        scratch_shapes=scratch,
        compiler_params=_params(2),
        name="outproj_ffn_step" if stepwise else "outproj_ffn",
    )(*args)


def _rope_tables(pos):
    def cos_sin(half):
        freqs = ROPE_BASE ** (-jnp.arange(half, dtype=F32) / half)
        ang = pos[:, None] * freqs[None, :]
        return jnp.cos(ang), jnp.sin(ang)

    def three(cos, sin, reps, width):
        zero = jnp.zeros_like(sin)
        tabs = [jnp.concatenate([cos, cos], 1), jnp.concatenate([-sin, zero], 1),
                jnp.concatenate([zero, sin], 1)]
        tabs = [jnp.tile(t, (1, reps)) for t in tabs]
        tabs = [jnp.pad(t, ((0, 0), (0, width - t.shape[1]))) for t in tabs]
        return jnp.stack(tabs)

    cos_r, sin_r = cos_sin(RET_DK // 2)
    cos_m, sin_m = cos_sin(MLA_ROPE // 2)
    return (three(cos_r, sin_r, RET_HEADS, RET_W), three(cos_m, sin_m, MLA_HEADS, QR_W),
            three(cos_m, sin_m, 1, LANE))


def _retention_tables(chunk):
    lg = jnp.log(1.0 - 2.0 ** (-5.0 - jnp.arange(RET_HEADS, dtype=F32)))
    idx = jnp.arange(chunk, dtype=F32)
    rel = idx[:, None] - idx[None, :]
    decay = jnp.where(rel >= 0, jnp.exp(lg[:, None, None] * jnp.maximum(rel, 0.0)), 0.0)
    per_lane = lambda t: jnp.repeat(t, RET_DK, axis=1)
    gq = per_lane(jnp.exp(lg[None, :] * (idx[:, None] + 1.0)))
    wk = per_lane(jnp.exp(lg[None, :] * (chunk - 1.0 - idx[:, None])))
    gl = jnp.broadcast_to(jnp.repeat(jnp.exp(lg * chunk), RET_DK)[:, None], (RET_W, RET_W))
    head = jnp.arange(RET_W) // RET_DK
    bd = (head[:, None] == head[None, :]).astype(F32)
    return dict(chunk=chunk, decay=decay, gq=gq, wk=wk, gl=gl, bd=bd, avg=bd / RET_DV,
                gam=jnp.exp(lg * 1.0))


def _block_diag(w):
    n, c, dd = w.shape
    out = jnp.zeros((n * c, n * dd), w.dtype)
    for i in range(n):
        out = out.at[i * c:(i + 1) * c, i * dd:(i + 1) * dd].set(w[i])
    return out


def _layer_weights(l, P):
    w_in = P["w_in"][l]
    bounds = [0, 256, 512, 768, 1024, 1280, 1408, 1440, 1696, 1952]
    kr_lo, kr_hi = bounds[6], bounds[7]
    w_in_p = jnp.concatenate(
        [w_in[:, :kr_lo], w_in[:, kr_hi:], w_in[:, kr_lo:kr_hi],
         jnp.zeros((w_in.shape[0], _N_IN_PAD - w_in.shape[1]), w_in.dtype)], axis=1)
    w_uq = P["mla_w_uq"][l]
    w_uk = P["mla_w_uk"][l]
    w_uv = P["mla_w_uv"][l]
    uk_pair = jnp.stack([_block_diag(jnp.stack([w_uk[:, 2 * p, :].T, w_uk[:, 2 * p + 1, :].T]))
                         for p in range(MLA_HEADS // 2)])
    uv_pair = jnp.stack([_block_diag(jnp.stack([w_uv[:, 2 * p, :], w_uv[:, 2 * p + 1, :]]))
                         for p in range(MLA_HEADS // 2)])
    row = lambda v: v.reshape(1, -1)
    return dict(
        norm1_w=row(P["norm1_w"][l]), w_in=w_in_p.astype(BF16),
        q_norm_w=row(P["mla_q_norm_w"][l]), kv_norm_w=row(P["mla_kv_norm_w"][l]),
        w_uq_nope=w_uq[:, :, :MLA_NOPE].reshape(MLA_Q_RANK, QN_W).astype(BF16),
        w_uq_rope=w_uq[:, :, MLA_NOPE:].reshape(MLA_Q_RANK, QR_W).astype(BF16),
        w_uk_pair=uk_pair.astype(BF16), w_uv_pair=uv_pair.astype(BF16),
        ret_gn_w=row(P["ret_gn_w"][l]),
        rg_conv_w=P["rg_conv_w"][l], rg_conv_b=row(P["rg_conv_b"][l]),
        rg_wa_bd=_block_diag(P["rg_w_a"][l]).astype(BF16), rg_b_a=row(P["rg_b_a"][l]),
        rg_wx_bd=_block_diag(P["rg_w_x"][l]).astype(BF16), rg_b_x=row(P["rg_b_x"][l]),
        rg_lambda=row(P["rg_lambda"][l]),
        w_out=P["w_out"][l].astype(BF16), norm2_w=row(P["norm2_w"][l]),
        ffn_w_up=P["ffn_w_up"][l].astype(BF16), ffn_conv_w=P["ffn_conv_w"][l],
        ffn_conv_b=row(P["ffn_conv_b"][l]), ffn_w_down=P["ffn_w_down"][l].astype(BF16),
    )


def _tile(n, pref):
    t = min(n, pref)
    assert n % t == 0, (n, pref)
    return t


def kernel(x_prompt, x_sample, cache_mla_latent, cache_mla_krope, page_table, state_ret, state_rg_conv,
           state_rglru, state_ffn_conv, norm1_w, w_in, ret_gn_w, mla_q_norm_w, mla_w_uq, mla_kv_norm_w,
           mla_w_uk, mla_w_uv, rg_conv_w, rg_conv_b, rg_w_a, rg_b_a, rg_w_x, rg_b_x, rg_lambda, w_out,
           norm2_w, ffn_w_up, ffn_conv_w, ffn_conv_b, ffn_w_down, final_norm_w):
    P = dict(norm1_w=norm1_w, w_in=w_in, ret_gn_w=ret_gn_w, mla_q_norm_w=mla_q_norm_w, mla_w_uq=mla_w_uq,
             mla_kv_norm_w=mla_kv_norm_w, mla_w_uk=mla_w_uk, mla_w_uv=mla_w_uv, rg_conv_w=rg_conv_w,
             rg_conv_b=rg_conv_b, rg_w_a=rg_w_a, rg_b_a=rg_b_a, rg_w_x=rg_w_x, rg_b_x=rg_b_x,
             rg_lambda=rg_lambda, w_out=w_out, norm2_w=norm2_w, ffn_w_up=ffn_w_up, ffn_conv_w=ffn_conv_w,
             ffn_conv_b=ffn_conv_b, ffn_w_down=ffn_w_down)
    depth = w_in.shape[0]
    B, S, D = x_prompt.shape
    DB, T, _ = x_sample.shape
    assert T == 1, "the sample path handles one new token per sequence"
    n_pages, page = page_table.shape[1], cache_mla_latent.shape[2]
    past_len = n_pages * page
    final_w = final_norm_w.reshape(1, D)
    weights = [_layer_weights(l, P) for l in range(depth)]

    chunk = RET_CHUNK if (S > RET_CHUNK and S % RET_CHUNK == 0) else S
    rt = _retention_tables(chunk)
    tabs_p = _rope_tables(jnp.arange(S, dtype=F32))
    tm_a, tt, tq, tm_e = _tile(S, 512), _tile(S, 512), _tile(S, 256), _tile(S, 512)
    ret_chunks_per_step = _tile(S // chunk, 4)
    kb = _tile(S, 512)
    x = x_prompt.reshape(B * S, D)
    p_new = []
    for l in range(depth):
        lw = weights[l]
        ret_in, qh, kcat, vt, p_lat, p_rope, rg_in = _inproj(x, B, tm_a, lw, tabs_p)
        y_ret, sbd = _retention_prompt(ret_in, B, lw, rt, ret_chunks_per_step)
        y_rg, p_rgc, p_rgh = _rglru_prompt(rg_in, B, tt, lw)
        y_mla = _attention_prompt(qh, kcat, vt, B, tq, kb, lw)
        x, p_ffc = _outproj_ffn(x, y_ret, y_mla, y_rg, lw, B, tm_e,
                                final_w if l == depth - 1 else None)
        s_ret = jnp.stack([sbd[:, h * RET_DK:(h + 1) * RET_DK, h * RET_DV:(h + 1) * RET_DV]
                           for h in range(RET_HEADS)], axis=1)
        p_new.append((p_lat.reshape(B, S, -1), p_rope.reshape(B, S, -1), s_ret, p_rgc,
                      p_rgh.reshape(B, -1), p_ffc))
    y_prompt = x.reshape(B, S, D)

    tabs_s = _rope_tables(jnp.full((DB,), past_len, dtype=F32))
    pages_per_chunk = _tile(n_pages, 64)
    krope_t = jnp.swapaxes(cache_mla_krope, 2, 3)
    state_ret_t = jnp.transpose(state_ret, (0, 2, 3, 4, 1))
    x = x_sample.reshape(DB, D)
    s_new = []
    for l in range(depth):
        lw = weights[l]
        ret_in, qh, kcat, _, s_lat, s_rope, rg_in = _inproj(x, 1, DB, lw, tabs_s)
        y_ret, s_ret = _retention_step(ret_in, state_ret_t, l, lw, rt)
        y_rg, s_rgh = _rglru_step(rg_in, state_rg_conv[l], state_rglru[l], lw)
        o_lat = _attention_paged(qh, kcat, cache_mla_latent, krope_t, page_table, l,
                                 pages_per_chunk)
        y_mla = _uv_step(o_lat, lw)
        x, gate = _outproj_ffn(x, y_ret, y_mla, y_rg, lw, 1, DB,
                               final_w if l == depth - 1 else None, step_buf=state_ffn_conv[l])
        s_rgc = jnp.concatenate([state_rg_conv[l][:, 1:], rg_in[:, None, 0:RG_WIDTH]], axis=1)
        s_ffc = jnp.concatenate([state_ffn_conv[l][:, 1:], gate[:, None, :]], axis=1)
        s_new.append((s_lat.reshape(DB, T, -1), s_rope.reshape(DB, T, -1), s_ret, s_rgc, s_rgh, s_ffc))
    y_sample = x.reshape(DB, T, D)

    stack = lambda items, i: jnp.stack([it[i] for it in items])
    return (y_prompt, y_sample, *[stack(p_new, i) for i in range(6)], *[stack(s_new, i) for i in range(6)])
```

```python
import functools

import jax
import jax.numpy as jnp
from jax import lax
from jax.experimental import pallas as pl
from jax.experimental.pallas import tpu as pltpu

RET_HEADS = 4
RET_DK = 64
RET_DV = 64
RET_CHUNK = 128
MLA_HEADS = 8
MLA_NOPE = 64
MLA_ROPE = 32
MLA_V = 64
MLA_Q_RANK = 256
MLA_KV_RANK = 128
MLA_SCALE = (MLA_NOPE + MLA_ROPE) ** -0.5
LOG2_E = 1.4426950408889634
Q_SCALE = MLA_SCALE * LOG2_E
RG_WIDTH = 256
RG_BLOCKS = 4
RG_CONV = 4
RG_C = 8.0
FFN_CONV = 3
ROPE_BASE = 10000.0
EPS = 1e-6
NEG_INF = -1e30

RET_W = RET_HEADS * RET_DK
QR_W = MLA_HEADS * MLA_ROPE
QN_W = MLA_HEADS * MLA_NOPE
KCAT_W = 256
LANE = 128
VMEM_LIMIT = 56 * 1024 * 1024
PAGED_SLOTS = 3

BF16 = jnp.bfloat16
F32 = jnp.float32

_OFF_RQ, _OFF_RK, _OFF_RV, _OFF_RG = 0, 256, 512, 768
_OFF_CQ, _OFF_CKV, _OFF_GX, _OFF_GG, _OFF_KR = 1024, 1280, 1408, 1664, 1920
_N_IN_PAD = 2048


def _params(n_axes):
    return pltpu.CompilerParams(dimension_semantics=("arbitrary",) * n_axes,
                                vmem_limit_bytes=VMEM_LIMIT)


def _const_spec(shape):
    n = len(shape)
    return pl.BlockSpec(shape, lambda *_: (0,) * n, pipeline_mode=pl.Buffered(1))


def _dot(a, b):
    return jnp.dot(a, b, preferred_element_type=F32)


def _dot_nt(a, b):
    return lax.dot_general(a, b, (((1,), (1,)), ((), ())), preferred_element_type=F32)


def _dot_tn(a, b):
    return lax.dot_general(a, b, (((0,), (0,)), ((), ())), preferred_element_type=F32)


def _rms(x, w):
    return x * lax.rsqrt(jnp.mean(x * x, axis=-1, keepdims=True) + EPS) * w


def _rope3(x, tab_ref, half):
    w = x.shape[-1]
    return (x * tab_ref[0] + pltpu.roll(x, w - half, 1) * tab_ref[1]
            + pltpu.roll(x, half, 1) * tab_ref[2])


def _inproj_kernel(x_ref, n1_ref, win_ref, tabr_ref, tabq_ref, tabk_ref, qnw_ref, wuqn_ref,
                   wuqr_ref, wukp_ref, kvnw_ref,
                   ret_ref, q_ref, kcat_ref, vt_ref, plat_ref, prope_ref, rg_ref):
    h = _rms(x_ref[...], n1_ref[...]).astype(BF16)
    z = _dot(h, win_ref[...])

    ret_ref[:, _OFF_RQ:_OFF_RQ + RET_W] = _rope3(z[:, _OFF_RQ:_OFF_RQ + RET_W], tabr_ref, RET_DK // 2)
    ret_ref[:, _OFF_RK:_OFF_RK + RET_W] = (
        _rope3(z[:, _OFF_RK:_OFF_RK + RET_W], tabr_ref, RET_DK // 2) * (RET_DK ** -0.5))
    ret_ref[:, _OFF_RV:_OFF_CQ] = z[:, _OFF_RV:_OFF_CQ]
    rg_ref[...] = z[:, _OFF_GX:_OFF_KR]

    ckvn = _rms(z[:, _OFF_CKV:_OFF_CKV + MLA_KV_RANK], kvnw_ref[...])
    plat_ref[...] = ckvn
    krr = _rope3(z[:, _OFF_KR:_OFF_KR + LANE], tabk_ref, MLA_ROPE // 2)
    prope_ref[...] = krr[:, :MLA_ROPE]
    kcat_ref[:, 0:MLA_KV_RANK] = ckvn.astype(BF16)
    kcat_ref[:, MLA_KV_RANK:KCAT_W] = krr.astype(BF16)
    vt_ref[...] = ckvn.T.astype(BF16)

    cqn = _rms(z[:, _OFF_CQ:_OFF_CQ + MLA_Q_RANK], qnw_ref[...]).astype(BF16)
    qn = _dot(cqn, wuqn_ref[...]).astype(BF16)
    qr = _rope3(_dot(cqn, wuqr_ref[...]), tabq_ref, MLA_ROPE // 2) * Q_SCALE
    lane = lax.broadcasted_iota(jnp.int32, (1, LANE), 1)
    heads_per_tile = LANE // MLA_ROPE
    for p in range(MLA_HEADS // 2):
        ql = _dot(qn[:, p * LANE:(p + 1) * LANE], wukp_ref[p]) * Q_SCALE
        for e in range(2):
            hd = 2 * p + e
            q_ref[hd, :, 0:MLA_KV_RANK] = ql[:, e * LANE:(e + 1) * LANE].astype(BF16)
            tile = qr[:, (hd // heads_per_tile) * LANE:(hd // heads_per_tile + 1) * LANE]
            sh = MLA_ROPE * (hd % heads_per_tile)
            if sh:
                tile = pltpu.roll(tile, LANE - sh, 1)
            q_ref[hd, :, MLA_KV_RANK:KCAT_W] = jnp.where(lane < MLA_ROPE, tile, 0.0).astype(BF16)


def _inproj(x2d, n_batch, tm, lw, tabs):
    rows, d = x2d.shape
    ns = rows // (n_batch * tm)
    tabr, tabq, tabk = tabs
    row_map = lambda s, b: (b * ns + s, 0)
    tab_map = lambda s, b: (0, s, 0)
    out_shape = (
        jax.ShapeDtypeStruct((rows, 4 * RET_W), F32),
        jax.ShapeDtypeStruct((MLA_HEADS, rows, KCAT_W), BF16),
        jax.ShapeDtypeStruct((rows, KCAT_W), BF16),
        jax.ShapeDtypeStruct((MLA_KV_RANK, rows), BF16),
        jax.ShapeDtypeStruct((rows, MLA_KV_RANK), F32),
        jax.ShapeDtypeStruct((rows, MLA_ROPE), F32),
        jax.ShapeDtypeStruct((rows, 2 * RG_WIDTH), F32),
    )
    return pl.pallas_call(
        _inproj_kernel,
        grid=(ns, n_batch),
        in_specs=[
            pl.BlockSpec((tm, d), row_map),
            _const_spec((1, d)),
            _const_spec((d, _N_IN_PAD)),
            pl.BlockSpec((3, tm, RET_W), tab_map),
            pl.BlockSpec((3, tm, QR_W), tab_map),
            pl.BlockSpec((3, tm, LANE), tab_map),
            _const_spec((1, MLA_Q_RANK)),
            _const_spec((MLA_Q_RANK, QN_W)),
            _const_spec((MLA_Q_RANK, QR_W)),
            _const_spec((MLA_HEADS // 2, LANE, 2 * MLA_KV_RANK)),
            _const_spec((1, MLA_KV_RANK)),
        ],
        out_specs=[
            pl.BlockSpec((tm, 4 * RET_W), row_map),
            pl.BlockSpec((MLA_HEADS, tm, KCAT_W), lambda s, b: (0, b * ns + s, 0)),
            pl.BlockSpec((tm, KCAT_W), row_map),
            pl.BlockSpec((MLA_KV_RANK, tm), lambda s, b: (0, b * ns + s)),
            pl.BlockSpec((tm, MLA_KV_RANK), row_map),
            pl.BlockSpec((tm, MLA_ROPE), row_map),
            pl.BlockSpec((tm, 2 * RG_WIDTH), row_map),
        ],
        out_shape=out_shape,
        compiler_params=_params(2),
        name="inproj",
    )(x2d, lw["norm1_w"], lw["w_in"], tabr, tabq, tabk, lw["q_norm_w"], lw["w_uq_nope"],
      lw["w_uq_rope"], lw["w_uk_pair"], lw["kv_norm_w"])


def _group_norm_gate(o, g, gnw, avg):
    mu = _dot(o, avg)
    d = o - mu
    var = _dot(d * d, avg)
    return d * lax.rsqrt(var + EPS) * gnw * (g * jax.nn.sigmoid(g))


def _ret_kernel(ret_ref, dec_ref, gq_ref, wk_ref, gl_ref, bd_ref, gnw_ref, avg_ref,
                y_ref, sbd_ref, s_scr):
    @pl.when(pl.program_id(1) == 0)
    def _():
        s_scr[...] = jnp.zeros_like(s_scr)

    L = dec_ref.shape[1]
    head_of_lane = lax.broadcasted_iota(jnp.int32, (1, RET_W), 1) // RET_DK
    s_old = s_scr[...]
    for cc in range(ret_ref.shape[0] // L):
        rows = pl.ds(cc * L, L)
        q = ret_ref[rows, _OFF_RQ:_OFF_RQ + RET_W]
        k = ret_ref[rows, _OFF_RK:_OFF_RK + RET_W]
        vb = ret_ref[rows, _OFF_RV:_OFF_RV + RET_W].astype(BF16)
        g = ret_ref[rows, _OFF_RG:_OFF_RG + RET_W]
        kb = k.astype(BF16)
        o = _dot(q.astype(BF16), s_old.astype(BF16)) * gq_ref[...]
        for h in range(RET_HEADS):
            mh = head_of_lane == h
            s = _dot_nt(jnp.where(mh, q, 0.0).astype(BF16), kb) * dec_ref[h]
            o = o + jnp.where(mh, _dot(s.astype(BF16), vb), 0.0)
        upd = _dot_tn((k * wk_ref[...]).astype(BF16), vb)
        s_old = gl_ref[...] * s_old + bd_ref[...] * upd
        y_ref[rows, :] = _group_norm_gate(o, g, gnw_ref[...], avg_ref[...])
    s_scr[...] = s_old
    sbd_ref[...] = s_old


def _retention_prompt(ret_in, n_batch, lw, rt, chunks_per_step):
    rows = ret_in.shape[0]
    L = rt["chunk"]
    tb = L * chunks_per_step
    nc = rows // (n_batch * tb)
    return pl.pallas_call(
        _ret_kernel,
        grid=(n_batch, nc),
        in_specs=[
            pl.BlockSpec((tb, 4 * RET_W), lambda b, c: (b * nc + c, 0)),
            _const_spec((RET_HEADS, L, L)),
            _const_spec((L, RET_W)),
            _const_spec((L, RET_W)),
            _const_spec((RET_W, RET_W)),
            _const_spec((RET_W, RET_W)),
            _const_spec((1, RET_W)),
            _const_spec((RET_W, RET_W)),
        ],
        out_specs=[
            pl.BlockSpec((tb, RET_W), lambda b, c: (b * nc + c, 0)),
            pl.BlockSpec((None, RET_W, RET_W), lambda b, c: (b, 0, 0)),
        ],
        out_shape=(jax.ShapeDtypeStruct((rows, RET_W), F32),
                   jax.ShapeDtypeStruct((n_batch, RET_W, RET_W), F32)),
        scratch_shapes=[pltpu.VMEM((RET_W, RET_W), F32)],
        compiler_params=_params(2),
        name="retention",
    )(ret_in, rt["decay"], rt["gq"], rt["wk"], rt["gl"], rt["bd"], lw["ret_gn_w"], rt["avg"])


def _ret_step_kernel(q_ref, k_ref, v_ref, g_ref, s0_ref, gam_ref, gnw_ref, y_ref, s_ref):
    gam = gam_ref[...]
    v = v_ref[...]
    o = jnp.zeros_like(v)
    for d in range(RET_DK):
        s_d = gam * s0_ref[d] + k_ref[d:d + 1, :] * v
        s_ref[d] = s_d
        o = o + q_ref[d:d + 1, :] * s_d
    mu = jnp.mean(o, axis=0, keepdims=True)
    dev = o - mu
    var = jnp.mean(dev * dev, axis=0, keepdims=True)
    g = g_ref[...]
    y_ref[...] = dev * lax.rsqrt(var + EPS) * gnw_ref[...] * (g * jax.nn.sigmoid(g))


def _retention_step(ret_in, s0_t, layer, lw, rt):
    n = ret_in.shape[0]
    heads_t = lambda off: ret_in[:, off:off + RET_W].T.reshape(RET_HEADS, RET_DK, n)
    gnw_t = jnp.broadcast_to(lw["ret_gn_w"].reshape(RET_HEADS, RET_DV, 1), (RET_HEADS, RET_DV, n))
    gam_t = jnp.broadcast_to(rt["gam"][:, None, None], (RET_HEADS, 1, n))
    vec = pl.BlockSpec((None, RET_DK, n), lambda h: (h, 0, 0))
    mat = pl.BlockSpec((None, RET_DK, RET_DV, n), lambda h: (h, 0, 0, 0))
    y_t, s_t = pl.pallas_call(
        _ret_step_kernel,
        grid=(RET_HEADS,),
        in_specs=[vec, vec, vec, vec,
                  pl.BlockSpec((None, None, RET_DK, RET_DV, n), lambda h: (layer, h, 0, 0, 0)),
                  pl.BlockSpec((None, 1, n), lambda h: (h, 0, 0)), vec],
        out_specs=[vec, mat],
        out_shape=(jax.ShapeDtypeStruct((RET_HEADS, RET_DV, n), F32),
                   jax.ShapeDtypeStruct((RET_HEADS, RET_DK, RET_DV, n), F32)),
        compiler_params=_params(1),
        name="retention_step",
    )(heads_t(_OFF_RQ), heads_t(_OFF_RK), heads_t(_OFF_RV), heads_t(_OFF_RG), s0_t, gam_t, gnw_t)
    return y_t.reshape(RET_W, n).T, jnp.transpose(s_t, (3, 0, 1, 2))


def _rg_gates(xc, wa_ref, ba_ref, wx_ref, bx_ref, lam_ref):
    xb = xc.astype(BF16)
    r = jax.nn.sigmoid(_dot(xb, wa_ref[...]) + ba_ref[...])
    i = jax.nn.sigmoid(_dot(xb, wx_ref[...]) + bx_ref[...])
    y = -lam_ref[...]
    softplus = jnp.maximum(y, 0.0) + jnp.log1p(jnp.exp(-jnp.abs(y)))
    log_a = -RG_C * r * softplus
    a = jnp.exp(log_a)
    u = jnp.sqrt(1.0 - jnp.exp(2.0 * log_a)) * i * xc
    return a, u


def _rg_kernel(rg_ref, cw_ref, cb_ref, wa_ref, ba_ref, wx_ref, bx_ref, lam_ref,
               y_ref, buf_ref, hlast_ref, xs_scr, h_scr):
    tt = rg_ref.shape[0]
    halo = 8

    @pl.when(pl.program_id(1) == 0)
    def _():
        xs_scr[0:halo, :] = jnp.zeros((halo, RG_WIDTH), F32)
        h_scr[...] = jnp.zeros_like(h_scr)

    gx = rg_ref[:, 0:RG_WIDTH]
    gg = rg_ref[:, RG_WIDTH:2 * RG_WIDTH]
    xs_scr[halo:halo + tt, :] = gx
    xc = cb_ref[...] + gx * cw_ref[RG_CONV - 1:RG_CONV, :]
    for j in range(1, RG_CONV):
        xc = xc + xs_scr[halo - j:halo - j + tt, :] * cw_ref[RG_CONV - 1 - j:RG_CONV - j, :]
    xs_scr[0:halo, :] = xs_scr[tt:tt + halo, :]
    buf_ref[...] = gx[tt - (RG_CONV - 1):tt, :]

    a, u = _rg_gates(xc, wa_ref, ba_ref, wx_ref, bx_ref, lam_ref)
    row = lax.broadcasted_iota(jnp.int32, (tt, 1), 0)
    step = 1
    while step < tt:
        keep = row >= step
        a_sh = jnp.where(keep, pltpu.roll(a, step, 0), 1.0)
        u_sh = jnp.where(keep, pltpu.roll(u, step, 0), 0.0)
        u = a * u_sh + u
        a = a * a_sh
        step *= 2
    h = u + a * h_scr[...]
    h_scr[...] = h[tt - 1:tt, :]
    hlast_ref[...] = h[tt - 1:tt, :]
    y_ref[...] = h * jax.nn.gelu(gg, approximate=True)


def _rglru_prompt(rg_in, n_batch, tt, lw):
    rows = rg_in.shape[0]
    nt = rows // (n_batch * tt)
    w = RG_WIDTH
    return pl.pallas_call(
        _rg_kernel,
        grid=(n_batch, nt),
        in_specs=[
            pl.BlockSpec((tt, 2 * w), lambda b, t: (b * nt + t, 0)),
            _const_spec((RG_CONV, w)), _const_spec((1, w)),
            _const_spec((w, w)), _const_spec((1, w)),
            _const_spec((w, w)), _const_spec((1, w)), _const_spec((1, w)),
        ],
        out_specs=[
            pl.BlockSpec((tt, w), lambda b, t: (b * nt + t, 0)),
            pl.BlockSpec((None, RG_CONV - 1, w), lambda b, t: (b, 0, 0)),
            pl.BlockSpec((None, 1, w), lambda b, t: (b, 0, 0)),
        ],
        out_shape=(jax.ShapeDtypeStruct((rows, w), F32),
                   jax.ShapeDtypeStruct((n_batch, RG_CONV - 1, w), F32),
                   jax.ShapeDtypeStruct((n_batch, 1, w), F32)),
        scratch_shapes=[pltpu.VMEM((tt + 8, w), F32), pltpu.VMEM((1, w), F32)],
        compiler_params=_params(2),
        name="rglru",
    )(rg_in, lw["rg_conv_w"], lw["rg_conv_b"], lw["rg_wa_bd"], lw["rg_b_a"], lw["rg_wx_bd"],
      lw["rg_b_x"], lw["rg_lambda"])


def _rg_step_kernel(rg_ref, b0_ref, b1_ref, b2_ref, h0_ref, cw_ref, cb_ref, wa_ref, ba_ref,
                    wx_ref, bx_ref, lam_ref, y_ref, h_ref):
    gx = rg_ref[:, 0:RG_WIDTH]
    gg = rg_ref[:, RG_WIDTH:2 * RG_WIDTH]
    xc = (cb_ref[...] + b0_ref[...] * cw_ref[0:1, :] + b1_ref[...] * cw_ref[1:2, :]
          + b2_ref[...] * cw_ref[2:3, :] + gx * cw_ref[3:4, :])
    a, u = _rg_gates(xc, wa_ref, ba_ref, wx_ref, bx_ref, lam_ref)
    h = u + a * h0_ref[...]
    h_ref[...] = h
    y_ref[...] = h * jax.nn.gelu(gg, approximate=True)


def _rglru_step(rg_in, buf0, h0, lw):
    n = rg_in.shape[0]
    w = RG_WIDTH
    full = lambda shape: pl.BlockSpec(shape, lambda i: (0,) * len(shape))
    return pl.pallas_call(
        _rg_step_kernel,
        grid=(1,),
        in_specs=[full((n, 2 * w)), full((n, w)), full((n, w)), full((n, w)), full((n, w)),
                  full((RG_CONV, w)), full((1, w)), full((w, w)), full((1, w)), full((w, w)),
                  full((1, w)), full((1, w))],
        out_specs=[full((n, w)), full((n, w))],
        out_shape=(jax.ShapeDtypeStruct((n, w), F32), jax.ShapeDtypeStruct((n, w), F32)),
        compiler_params=_params(1),
        name="rglru_step",
    )(rg_in, buf0[:, 0], buf0[:, 1], buf0[:, 2], h0, lw["rg_conv_w"], lw["rg_conv_b"],
      lw["rg_wa_bd"], lw["rg_b_a"], lw["rg_wx_bd"], lw["rg_b_x"], lw["rg_lambda"])


def _uv_project_t(o_t, tq, wuv_ref, out_ref):
    for p in range(MLA_HEADS // 2):
        pair_t = jnp.concatenate([o_t[:, (2 * p) * tq:(2 * p + 1) * tq],
                                  o_t[:, (2 * p + 1) * tq:(2 * p + 2) * tq]], axis=0).astype(BF16)
        out_ref[:, p * LANE:(p + 1) * LANE] = _dot_tn(pair_t, wuv_ref[p])


def _attn_kernel(q_ref, k_ref, vt_ref, wuv_ref, o_ref, sa_scr, sb_scr, m_scr, l_scr, acc_scr, *, kb):
    tq = q_ref.shape[1]
    assert tq & (tq - 1) == 0 and kb % tq == 0
    m_cols = MLA_HEADS * tq
    i = pl.program_id(1)
    q = q_ref[...].reshape(m_cols, KCAT_W)
    m_scr[...] = jnp.full_like(m_scr, NEG_INF)
    l_scr[...] = jnp.zeros_like(l_scr)
    acc_scr[...] = jnp.zeros_like(acc_scr)

    def scores(dst, j):
        dst[...] = _dot_nt(k_ref[pl.ds(pl.multiple_of(j * kb, kb), kb), :], q)

    def absorb(src, j, masked):
        s = src[...]
        if masked:
            krow = lax.broadcasted_iota(jnp.int32, (kb, m_cols), 0) + (j * kb - i * tq)
            qcol = lax.broadcasted_iota(jnp.int32, (kb, m_cols), 1) & (tq - 1)
            s = jnp.where(krow <= qcol, s, NEG_INF)
        m_prev = m_scr[...]
        m_next = jnp.maximum(m_prev, jnp.max(s, axis=0, keepdims=True))
        p = jnp.exp2(s - m_next)
        alpha = jnp.exp2(m_prev - m_next)
        l_scr[...] = alpha * l_scr[...] + jnp.sum(p, axis=0, keepdims=True)
        vt = vt_ref[:, pl.ds(pl.multiple_of(j * kb, kb), kb)]
        acc_scr[...] = alpha * acc_scr[...] + _dot(vt, p.astype(BF16))
        m_scr[...] = m_next

    n_blocks = (i * tq) // kb + 1
    n_pairs = (n_blocks - 1) // 2
    scores(sa_scr, 0)

    def pair(t, carry):
        scores(sb_scr, 2 * t + 1)
        absorb(sa_scr, 2 * t, False)
        scores(sa_scr, 2 * t + 2)
        absorb(sb_scr, 2 * t + 1, False)
        return carry

    lax.fori_loop(0, n_pairs, pair, 0)
    odd_left = (n_blocks - 1) - 2 * n_pairs

    @pl.when(odd_left == 0)
    def _():
        absorb(sa_scr, n_blocks - 1, True)

    @pl.when(odd_left == 1)
    def _():
        scores(sb_scr, n_blocks - 1)
        absorb(sa_scr, n_blocks - 2, False)
        absorb(sb_scr, n_blocks - 1, True)

    o_t = acc_scr[...] / l_scr[...]
    _uv_project_t(o_t, tq, wuv_ref, o_ref)


def _attention_prompt(qh, kcat, vt, n_batch, tq, kb, lw):
    rows = kcat.shape[0]
    s_len = rows // n_batch
    nq = s_len // tq
    m_cols = MLA_HEADS * tq
    return pl.pallas_call(
        functools.partial(_attn_kernel, kb=kb),
        grid=(n_batch, nq),
        in_specs=[
            pl.BlockSpec((MLA_HEADS, tq, KCAT_W), lambda b, i: (0, b * nq + i, 0)),
            pl.BlockSpec((s_len, KCAT_W), lambda b, i: (b, 0)),
            pl.BlockSpec((MLA_KV_RANK, s_len), lambda b, i: (0, b)),
            _const_spec((MLA_HEADS // 2, 2 * MLA_KV_RANK, LANE)),
        ],
        out_specs=pl.BlockSpec((tq, MLA_HEADS * MLA_V), lambda b, i: (b * nq + i, 0)),
        out_shape=jax.ShapeDtypeStruct((rows, MLA_HEADS * MLA_V), F32),
        scratch_shapes=[pltpu.VMEM((kb, m_cols), F32), pltpu.VMEM((kb, m_cols), F32),
                        pltpu.VMEM((1, m_cols), F32), pltpu.VMEM((1, m_cols), F32),
                        pltpu.VMEM((MLA_KV_RANK, m_cols), F32)],
        compiler_params=_params(2),
        name="attention",
    )(qh, kcat, vt, lw["w_uv_pair"])


def _paged_kernel(pt_ref, q_ref, kself_ref, lat_hbm, krt_hbm, o_ref,
                  lat_buf, krt_buf, sem, m_scr, l_scr, acc_scr, *, layer, pages_per_chunk,
                  chunks_per_seq, sub_keys):
    g = pl.program_id(0)
    page = lat_hbm.shape[2]
    chunk_keys = pages_per_chunk * page
    n_slots = lat_buf.shape[0]
    ahead = n_slots - 1

    last = pl.num_programs(0) - 1
    slot = g % n_slots
    fill = (g + ahead) % n_slots
    n_sub = chunk_keys // sub_keys
    pages_per_sub = pages_per_chunk // n_sub

    def page_copies(pid, dst_slot, j):
        cols = pl.ds(j * page, page)
        return (pltpu.make_async_copy(lat_hbm.at[layer, pid], lat_buf.at[dst_slot, cols, :], sem.at[dst_slot, 0]),
                pltpu.make_async_copy(krt_hbm.at[layer, pid], krt_buf.at[dst_slot, :, cols], sem.at[dst_slot, 1]))

    def start_pages(step, dst_slot, pages):
        for j in pages:
            for prio, cp in enumerate(page_copies(pt_ref[step * pages_per_chunk + j], dst_slot, j)):
                cp.start(priority=prio)

    def wait_chunk(dst_slot):
        for j in range(pages_per_chunk):
            for cp in page_copies(0, dst_slot, j):
                cp.wait()

    @pl.when(g == 0)
    def _():
        for a in range(ahead):
            start_pages(jnp.minimum(a, last), a, range(pages_per_chunk))

    wait_chunk(slot)
    c = g % chunks_per_seq

    @pl.when(c == 0)
    def _():
        m_scr[...] = jnp.full_like(m_scr, NEG_INF)
        l_scr[...] = jnp.zeros_like(l_scr)
        acc_scr[...] = jnp.zeros_like(acc_scr)

    q = q_ref[...]
    q_lat = q[:, 0:MLA_KV_RANK]
    q_rope = q[:, MLA_KV_RANK:MLA_KV_RANK + MLA_ROPE]
    nxt = jnp.minimum(g + ahead, last)
    lats, scores = [], []
    for sb in range(n_sub):
        start_pages(nxt, fill, range(sb * pages_per_sub, (sb + 1) * pages_per_sub))
        keys = pl.ds(sb * sub_keys, sub_keys)
        lat = lat_buf[slot, keys, :].astype(BF16)
        krt = krt_buf[slot, :, keys].astype(BF16)
        lats.append(lat)
        scores.append(_dot_nt(q_lat, lat) + _dot(q_rope, krt))
    m_prev = m_scr[...]
    m_cur = functools.reduce(jnp.maximum, [jnp.max(s, axis=1, keepdims=True) for s in scores])
    m_next = jnp.maximum(m_prev, m_cur)
    alpha = jnp.exp2(m_prev - m_next)
    l_new = alpha * l_scr[...]
    acc = alpha * acc_scr[...]
    for s, lat in zip(scores, lats):
        p = jnp.exp2(s - m_next)
        l_new = l_new + jnp.sum(p, axis=1, keepdims=True)
        acc = acc + _dot(p.astype(BF16), lat)
    m_scr[...] = m_next
    l_scr[...] = l_new
    acc_scr[...] = acc

    @pl.when(g == last)
    def _():
        for a in range(1, n_slots):
            wait_chunk((g + a) % n_slots)

    @pl.when(c == chunks_per_seq - 1)
    def _():
        kself = kself_ref[...].astype(F32)
        s_self = jnp.sum(q.astype(F32) * kself, axis=1, keepdims=True)
        m_prev = m_scr[...]
        m_fin = jnp.maximum(m_prev, s_self)
        alpha = jnp.exp2(m_prev - m_fin)
        p_self = jnp.exp2(s_self - m_fin)
        l_fin = alpha * l_scr[...] + p_self
        acc_fin = alpha * acc_scr[...] + p_self * kself[:, 0:MLA_KV_RANK]
        o_ref[...] = acc_fin / l_fin


def _attention_paged(qh, kcat, pool_lat, pool_rope_t, page_table, layer, pages_per_chunk):
    n = kcat.shape[0]
    n_pages = page_table.shape[1]
    page, rank = pool_lat.shape[2], pool_lat.shape[3]
    rope = pool_rope_t.shape[2]
    ppc = pages_per_chunk
    cps = n_pages // ppc
    chunk_keys = ppc * page
    q3 = jnp.transpose(qh, (1, 0, 2))
    kself = kcat.reshape(n, 1, KCAT_W)
    pt_flat = page_table.reshape(-1)
    seq_map = lambda g, pt: (g // cps, 0, 0)
    grid_spec = pltpu.PrefetchScalarGridSpec(
        num_scalar_prefetch=1,
        grid=(n * cps,),
        in_specs=[pl.BlockSpec((None, MLA_HEADS, KCAT_W), seq_map),
                  pl.BlockSpec((None, 1, KCAT_W), seq_map),
                  pl.BlockSpec(memory_space=pl.ANY),
                  pl.BlockSpec(memory_space=pl.ANY)],
        out_specs=pl.BlockSpec((None, MLA_HEADS, MLA_KV_RANK), seq_map),
        scratch_shapes=[pltpu.VMEM((PAGED_SLOTS, chunk_keys, rank), F32),
                        pltpu.VMEM((PAGED_SLOTS, rope, chunk_keys), F32),
                        pltpu.SemaphoreType.DMA((PAGED_SLOTS, 2)),
                        pltpu.VMEM((MLA_HEADS, 1), F32), pltpu.VMEM((MLA_HEADS, 1), F32),
                        pltpu.VMEM((MLA_HEADS, MLA_KV_RANK), F32)],
    )
    return pl.pallas_call(
        functools.partial(_paged_kernel, layer=layer, pages_per_chunk=ppc, chunks_per_seq=cps,
                          sub_keys=min(chunk_keys, 2048)),
        grid_spec=grid_spec,
        out_shape=jax.ShapeDtypeStruct((n, MLA_HEADS, MLA_KV_RANK), F32),
        compiler_params=_params(1),
        name="attention_paged",
    )(pt_flat, q3, kself, pool_lat, pool_rope_t)


def _uv_kernel(o_ref, wuv_ref, y_ref):
    for p in range(MLA_HEADS // 2):
        pair = o_ref[:, 2 * p * MLA_KV_RANK:(2 * p + 2) * MLA_KV_RANK].astype(BF16)
        y_ref[:, p * LANE:(p + 1) * LANE] = _dot(pair, wuv_ref[p])


def _uv_step(o_lat, lw):
    n = o_lat.shape[0]
    o2 = o_lat.reshape(n, MLA_HEADS * MLA_KV_RANK)
    full = lambda shape: pl.BlockSpec(shape, lambda i: (0,) * len(shape))
    return pl.pallas_call(
        _uv_kernel,
        grid=(1,),
        in_specs=[full(o2.shape), full((MLA_HEADS // 2, 2 * MLA_KV_RANK, LANE))],
        out_specs=full((n, MLA_HEADS * MLA_V)),
        out_shape=jax.ShapeDtypeStruct((n, MLA_HEADS * MLA_V), F32),
        compiler_params=_params(1),
        name="uv_step",
    )(o2, lw["w_uv_pair"])


def _ff_bounds(d_ff, max_tiles=4, tile=256):
    if d_ff % tile:
        return (0, d_ff)
    tiles = d_ff // tile
    n = -(-tiles // max_tiles)
    sizes = [tiles // n + (1 if c >= n - tiles % n else 0) for c in range(n)]
    bounds = [0]
    for s in sizes:
        bounds.append(bounds[-1] + s * tile)
    return tuple(bounds)


def _ffn_kernel(*refs, stepwise, final_norm, bounds):
    x_ref, yret_ref, ymla_ref, yrg_ref, wout_ref, n2_ref, wup_ref, cw_ref, cb_ref, wdn_ref = refs[:10]
    pos = 10
    fn_ref = None
    if final_norm:
        fn_ref = refs[pos]
        pos += 1
    if stepwise:
        bm2_ref, bm1_ref = refs[pos:pos + 2]
        out_ref, gate_ref = refs[pos + 2:pos + 4]
    else:
        out_ref, tail_ref, g_scr, carry_scr = refs[pos:pos + 4]
    tm = x_ref.shape[0]
    d_ff = wdn_ref.shape[0]
    halo = 8

    if not stepwise:
        @pl.when(pl.program_id(1) == 0)
        def _():
            carry_scr[...] = jnp.zeros_like(carry_scr)

    mix = jnp.concatenate([yret_ref[...], ymla_ref[...], yrg_ref[...]], axis=1).astype(BF16)
    x1 = x_ref[...] + _dot(mix, wout_ref[...])
    h2 = _rms(x1, n2_ref[...]).astype(BF16)
    acc = jnp.zeros_like(x1)
    for c in range(len(bounds) - 1):
        lo, hi = bounds[c], bounds[c + 1]
        tf = hi - lo
        gate = _dot(h2, wup_ref[:, lo:hi])
        val = _dot(h2, wup_ref[:, d_ff + lo:d_ff + hi])
        if stepwise:
            gm2 = bm2_ref[:, lo:hi]
            gm1 = bm1_ref[:, lo:hi]
            gate_ref[:, lo:hi] = gate
        else:
            g_scr[0:halo, 0:tf] = carry_scr[c, :, 0:tf]
            g_scr[halo:halo + tm, 0:tf] = gate
            gm1 = g_scr[halo - 1:halo - 1 + tm, 0:tf]
            gm2 = g_scr[halo - 2:halo - 2 + tm, 0:tf]
            carry_scr[c, :, 0:tf] = g_scr[tm:tm + halo, 0:tf]
            tail_ref[:, lo:hi] = gate[tm - (FFN_CONV - 1):tm, :]
        gc = (cb_ref[:, lo:hi] + gm2 * cw_ref[0:1, lo:hi] + gm1 * cw_ref[1:2, lo:hi]
              + gate * cw_ref[2:3, lo:hi])
        act = (gc * jax.nn.sigmoid(gc) * val).astype(BF16)
        acc = acc + _dot(act, wdn_ref[lo:hi, :])
    x2 = x1 + acc
    out_ref[...] = _rms(x2, fn_ref[...]) if final_norm else x2


def _outproj_ffn(x2d, y_ret, y_mla, y_rg, lw, n_batch, tm, final_w, step_buf=None):
    rows, d = x2d.shape
    d_ff = lw["ffn_w_down"].shape[0]
    bounds = _ff_bounds(d_ff)
    n_chunks = len(bounds) - 1
    tf = max(bounds[c + 1] - bounds[c] for c in range(n_chunks))
    stepwise = step_buf is not None
    final_norm = final_w is not None
    nt = rows // (n_batch * tm)
    row_map = lambda b, t: (b * nt + t, 0)
    in_specs = [
        pl.BlockSpec((tm, d), row_map),
        pl.BlockSpec((tm, RET_W), row_map),
        pl.BlockSpec((tm, MLA_HEADS * MLA_V), row_map),
        pl.BlockSpec((tm, RG_WIDTH), row_map),
        _const_spec((d, d)), _const_spec((1, d)), _const_spec((d, 2 * d_ff)),
        _const_spec((FFN_CONV, d_ff)), _const_spec((1, d_ff)), _const_spec((d_ff, d)),
    ]
    args = [x2d, y_ret, y_mla, y_rg, lw["w_out"], lw["norm2_w"], lw["ffn_w_up"], lw["ffn_conv_w"],
            lw["ffn_conv_b"], lw["ffn_w_down"]]
    if final_norm:
        in_specs.append(_const_spec((1, d)))
        args.append(final_w)
    if stepwise:
        in_specs += [pl.BlockSpec((tm, d_ff), row_map)] * 2
        args += [step_buf[:, 0], step_buf[:, 1]]
        out_specs = [pl.BlockSpec((tm, d), row_map), pl.BlockSpec((tm, d_ff), row_map)]
        out_shape = (jax.ShapeDtypeStruct((rows, d), F32), jax.ShapeDtypeStruct((rows, d_ff), F32))
        scratch = []
    else:
        out_specs = [pl.BlockSpec((tm, d), row_map),
                     pl.BlockSpec((None, FFN_CONV - 1, d_ff), lambda b, t: (b, 0, 0))]
        out_shape = (jax.ShapeDtypeStruct((rows, d), F32),
                     jax.ShapeDtypeStruct((n_batch, FFN_CONV - 1, d_ff), F32))
        scratch = [pltpu.VMEM((tm + 8, tf), F32), pltpu.VMEM((n_chunks, 8, tf), F32)]
    return pl.pallas_call(
        functools.partial(_ffn_kernel, stepwise=stepwise, final_norm=final_norm, bounds=bounds),
        grid=(n_batch, nt),
        in_specs=in_specs,
        out_specs=out_specs,
        out_shape=out_shape,
        scratch_shapes=scratch,
        compiler_params=_params(2),
        name="outproj_ffn_step" if stepwise else "outproj_ffn",
    )(*args)


def _rope_tables(pos):
    def cos_sin(half):
        freqs = ROPE_BASE ** (-jnp.arange(half, dtype=F32) / half)
        ang = pos[:, None] * freqs[None, :]
        return jnp.cos(ang), jnp.sin(ang)

    def three(cos, sin, reps, width):
        zero = jnp.zeros_like(sin)
        tabs = [jnp.concatenate([cos, cos], 1), jnp.concatenate([-sin, zero], 1),
                jnp.concatenate([zero, sin], 1)]
        tabs = [jnp.tile(t, (1, reps)) for t in tabs]
        tabs = [jnp.pad(t, ((0, 0), (0, width - t.shape[1]))) for t in tabs]
        return jnp.stack(tabs)

    cos_r, sin_r = cos_sin(RET_DK // 2)
    cos_m, sin_m = cos_sin(MLA_ROPE // 2)
    return (three(cos_r, sin_r, RET_HEADS, RET_W), three(cos_m, sin_m, MLA_HEADS, QR_W),
            three(cos_m, sin_m, 1, LANE))


def _retention_tables(chunk):
    lg = jnp.log(1.0 - 2.0 ** (-5.0 - jnp.arange(RET_HEADS, dtype=F32)))
    idx = jnp.arange(chunk, dtype=F32)
    rel = idx[:, None] - idx[None, :]
    decay = jnp.where(rel >= 0, jnp.exp(lg[:, None, None] * jnp.maximum(rel, 0.0)), 0.0)
    per_lane = lambda t: jnp.repeat(t, RET_DK, axis=1)
    gq = per_lane(jnp.exp(lg[None, :] * (idx[:, None] + 1.0)))
    wk = per_lane(jnp.exp(lg[None, :] * (chunk - 1.0 - idx[:, None])))
    gl = jnp.broadcast_to(jnp.repeat(jnp.exp(lg * chunk), RET_DK)[:, None], (RET_W, RET_W))
    head = jnp.arange(RET_W) // RET_DK
    bd = (head[:, None] == head[None, :]).astype(F32)
    return dict(chunk=chunk, decay=decay, gq=gq, wk=wk, gl=gl, bd=bd, avg=bd / RET_DV,
                gam=jnp.exp(lg * 1.0))


def _block_diag(w):
    n, c, dd = w.shape
    out = jnp.zeros((n * c, n * dd), w.dtype)
    for i in range(n):
        out = out.at[i * c:(i + 1) * c, i * dd:(i + 1) * dd].set(w[i])
    return out


def _layer_weights(l, P):
    w_in = P["w_in"][l]
    bounds = [0, 256, 512, 768, 1024, 1280, 1408, 1440, 1696, 1952]
    kr_lo, kr_hi = bounds[6], bounds[7]
    w_in_p = jnp.concatenate(
        [w_in[:, :kr_lo], w_in[:, kr_hi:], w_in[:, kr_lo:kr_hi],
         jnp.zeros((w_in.shape[0], _N_IN_PAD - w_in.shape[1]), w_in.dtype)], axis=1)
    w_uq = P["mla_w_uq"][l]
    w_uk = P["mla_w_uk"][l]
    w_uv = P["mla_w_uv"][l]
    uk_pair = jnp.stack([_block_diag(jnp.stack([w_uk[:, 2 * p, :].T, w_uk[:, 2 * p + 1, :].T]))
                         for p in range(MLA_HEADS // 2)])
    uv_pair = jnp.stack([_block_diag(jnp.stack([w_uv[:, 2 * p, :], w_uv[:, 2 * p + 1, :]]))
                         for p in range(MLA_HEADS // 2)])
    row = lambda v: v.reshape(1, -1)
    return dict(
        norm1_w=row(P["norm1_w"][l]), w_in=w_in_p.astype(BF16),
        q_norm_w=row(P["mla_q_norm_w"][l]), kv_norm_w=row(P["mla_kv_norm_w"][l]),
        w_uq_nope=w_uq[:, :, :MLA_NOPE].reshape(MLA_Q_RANK, QN_W).astype(BF16),
        w_uq_rope=w_uq[:, :, MLA_NOPE:].reshape(MLA_Q_RANK, QR_W).astype(BF16),
        w_uk_pair=uk_pair.astype(BF16), w_uv_pair=uv_pair.astype(BF16),
        ret_gn_w=row(P["ret_gn_w"][l]),
        rg_conv_w=P["rg_conv_w"][l], rg_conv_b=row(P["rg_conv_b"][l]),
        rg_wa_bd=_block_diag(P["rg_w_a"][l]).astype(BF16), rg_b_a=row(P["rg_b_a"][l]),
        rg_wx_bd=_block_diag(P["rg_w_x"][l]).astype(BF16), rg_b_x=row(P["rg_b_x"][l]),
        rg_lambda=row(P["rg_lambda"][l]),
        w_out=P["w_out"][l].astype(BF16), norm2_w=row(P["norm2_w"][l]),
        ffn_w_up=P["ffn_w_up"][l].astype(BF16), ffn_conv_w=P["ffn_conv_w"][l],
        ffn_conv_b=row(P["ffn_conv_b"][l]), ffn_w_down=P["ffn_w_down"][l].astype(BF16),
    )


def _tile(n, pref):
    t = min(n, pref)
    assert n % t == 0, (n, pref)
    return t


def kernel(x_prompt, x_sample, cache_mla_latent, cache_mla_krope, page_table, state_ret, state_rg_conv,
           state_rglru, state_ffn_conv, norm1_w, w_in, ret_gn_w, mla_q_norm_w, mla_w_uq, mla_kv_norm_w,
           mla_w_uk, mla_w_uv, rg_conv_w, rg_conv_b, rg_w_a, rg_b_a, rg_w_x, rg_b_x, rg_lambda, w_out,
           norm2_w, ffn_w_up, ffn_conv_w, ffn_conv_b, ffn_w_down, final_norm_w):
    P = dict(norm1_w=norm1_w, w_in=w_in, ret_gn_w=ret_gn_w, mla_q_norm_w=mla_q_norm_w, mla_w_uq=mla_w_uq,
             mla_kv_norm_w=mla_kv_norm_w, mla_w_uk=mla_w_uk, mla_w_uv=mla_w_uv, rg_conv_w=rg_conv_w,
             rg_conv_b=rg_conv_b, rg_w_a=rg_w_a, rg_b_a=rg_b_a, rg_w_x=rg_w_x, rg_b_x=rg_b_x,
             rg_lambda=rg_lambda, w_out=w_out, norm2_w=norm2_w, ffn_w_up=ffn_w_up, ffn_conv_w=ffn_conv_w,
             ffn_conv_b=ffn_conv_b, ffn_w_down=ffn_w_down)
    depth = w_in.shape[0]
    B, S, D = x_prompt.shape
    DB, T, _ = x_sample.shape
    assert T == 1, "the sample path handles one new token per sequence"
    n_pages, page = page_table.shape[1], cache_mla_latent.shape[2]
    past_len = n_pages * page
    final_w = final_norm_w.reshape(1, D)
    weights = [_layer_weights(l, P) for l in range(depth)]

    chunk = RET_CHUNK if (S > RET_CHUNK and S % RET_CHUNK == 0) else S
    rt = _retention_tables(chunk)
    tabs_p = _rope_tables(jnp.arange(S, dtype=F32))
    tm_a, tt, tq, tm_e = _tile(S, 512), _tile(S, 512), _tile(S, 256), _tile(S, 512)
    ret_chunks_per_step = _tile(S // chunk, 4)
    kb = _tile(S, 512)
    x = x_prompt.reshape(B * S, D)
    p_new = []
    for l in range(depth):
        lw = weights[l]
        ret_in, qh, kcat, vt, p_lat, p_rope, rg_in = _inproj(x, B, tm_a, lw, tabs_p)
        y_ret, sbd = _retention_prompt(ret_in, B, lw, rt, ret_chunks_per_step)
        y_rg, p_rgc, p_rgh = _rglru_prompt(rg_in, B, tt, lw)
        y_mla = _attention_prompt(qh, kcat, vt, B, tq, kb, lw)
        x, p_ffc = _outproj_ffn(x, y_ret, y_mla, y_rg, lw, B, tm_e,
                                final_w if l == depth - 1 else None)
        s_ret = jnp.stack([sbd[:, h * RET_DK:(h + 1) * RET_DK, h * RET_DV:(h + 1) * RET_DV]
                           for h in range(RET_HEADS)], axis=1)
        p_new.append((p_lat.reshape(B, S, -1), p_rope.reshape(B, S, -1), s_ret, p_rgc,
                      p_rgh.reshape(B, -1), p_ffc))
    y_prompt = x.reshape(B, S, D)

    tabs_s = _rope_tables(jnp.full((DB,), past_len, dtype=F32))
    pages_per_chunk = _tile(n_pages, 64)
    krope_t = jnp.swapaxes(cache_mla_krope, 2, 3)
    state_ret_t = jnp.transpose(state_ret, (0, 2, 3, 4, 1))
    x = x_sample.reshape(DB, D)
    s_new = []
    for l in range(depth):
        lw = weights[l]
        ret_in, qh, kcat, _, s_lat, s_rope, rg_in = _inproj(x, 1, DB, lw, tabs_s)
        y_ret, s_ret = _retention_step(ret_in, state_ret_t, l, lw, rt)
        y_rg, s_rgh = _rglru_step(rg_in, state_rg_conv[l], state_rglru[l], lw)
        o_lat = _attention_paged(qh, kcat, cache_mla_latent, krope_t, page_table, l,
                                 pages_per_chunk)
        y_mla = _uv_step(o_lat, lw)
        x, gate = _outproj_ffn(x, y_ret, y_mla, y_rg, lw, 1, DB,
                               final_w if l == depth - 1 else None, step_buf=state_ffn_conv[l])
        s_rgc = jnp.concatenate([state_rg_conv[l][:, 1:], rg_in[:, None, 0:RG_WIDTH]], axis=1)
        s_ffc = jnp.concatenate([state_ffn_conv[l][:, 1:], gate[:, None, :]], axis=1)
        s_new.append((s_lat.reshape(DB, T, -1), s_rope.reshape(DB, T, -1), s_ret, s_rgc, s_rgh, s_ffc))
    y_sample = x.reshape(DB, T, D)

    stack = lambda items, i: jnp.stack([it[i] for it in items])
    return (y_prompt, y_sample, *[stack(p_new, i) for i in range(6)], *[stack(s_new, i) for i in range(6)])
```
